```python
import math
import jax, jax.numpy as jnp
from jax import lax
import numpy as np

D_MODEL = 4096
BATCH = 4
SEQ = 2048
DEPTH = 4
DEC_BATCH = 32
DEC_SEQ = 8
PAST_LEN = 8192
PAGE_SIZE = 128

GLA_HEADS = 4
GLA_DK = 256
GLA_DV = 512
GLA_KEY = GLA_HEADS * GLA_DK
GLA_VAL = GLA_HEADS * GLA_DV
GLA_RANK = 16
GLA_TAU = 16.0
GLA_CHUNK = 64
SWA_HEADS = 32
SWA_KV_HEADS = 4
SWA_HEAD_DIM = 64
SWA_GROUP = SWA_HEADS // SWA_KV_HEADS
WINDOW = 128
SWA_Q = SWA_HEADS * SWA_HEAD_DIM
SWA_KV = SWA_KV_HEADS * SWA_HEAD_DIM
POOL_WINDOWS = (2, 4, 8, 16)
POOL_GROUPS = 4
POOL_GROUP_DIM = 512
POOL_DIM = POOL_GROUPS * POOL_GROUP_DIM
POOL_MAX_WIN = 16
D_FF = 11008
CONV_WIDTH = 3
PLE_DIM = 256
N_BRANCH = 3
EPS = 1e-6

IN_SIZES = (GLA_KEY, GLA_KEY, GLA_VAL, GLA_VAL, GLA_RANK, SWA_Q, SWA_KV, SWA_KV, POOL_DIM)
N_IN = 2 * GLA_KEY + 2 * GLA_VAL + GLA_RANK + SWA_Q + 2 * SWA_KV + POOL_DIM

kernel_name = 'hybrid_gla_swa_pool_convffn_step'


def _rmsnorm(x, g):
    xf = x.astype(jnp.float32)
    y = xf * lax.rsqrt(jnp.mean(xf * xf, axis=-1, keepdims=True) + EPS)
    return (y * g.astype(jnp.float32)).astype(x.dtype)


def _split_in(z):
    idx = np.cumsum(IN_SIZES)[:-1].tolist()
    return jnp.split(z, idx, axis=-1)


def _gla_scan(q, k, v, log_a, s0):
    B, T = q.shape[:2]
    C = math.gcd(T, GLA_CHUNK)
    N = T // C

    def blk(a):
        return jnp.moveaxis(a.reshape(B, N, C, *a.shape[2:]), 1, 0)

    causal = jnp.tril(jnp.ones((C, C), dtype=bool))

    def step(S, inp):
        qn, kn, vn, gn = inp
        b = jnp.cumsum(gn, axis=1)
        b_last = b[:, -1]
        q_t = qn * jnp.exp(b)
        k_t = kn * jnp.exp(-b)
        att = jnp.where(causal, jnp.einsum('bihd,bjhd->bhij', q_t, k_t), 0.0)
        o = jnp.einsum('bhij,bjhe->bihe', att, vn) + jnp.einsum('bihd,bhde->bihe', q_t, S)
        k_end = kn * jnp.exp(b_last[:, None] - b)
        S = jnp.exp(b_last)[..., None] * S + jnp.einsum('bjhd,bjhe->bhde', k_end, vn)
        return S, o

    S, o = lax.scan(step, s0, (blk(q), blk(k), blk(v), blk(log_a)))
    o = jnp.moveaxis(o, 0, 1).reshape(B, T, GLA_HEADS, GLA_DV)
    return o, S


def _gla_branch(gq, gk, gv, gr, glr, w_a2, b_a, g_out, s0):
    B, T, _ = gv.shape
    f32 = jnp.float32
    q = gq.reshape(B, T, GLA_HEADS, GLA_DK).astype(f32) * (GLA_DK ** -0.5)
    k = gk.reshape(B, T, GLA_HEADS, GLA_DK).astype(f32)
    v = gv.reshape(B, T, GLA_HEADS, GLA_DV).astype(f32)
    log_a = jax.nn.log_sigmoid((glr @ w_a2 + b_a).astype(f32)) / GLA_TAU
    o, s = _gla_scan(q, k, v, log_a.reshape(B, T, GLA_HEADS, GLA_DK), s0.astype(f32))
    o = _rmsnorm(o, g_out).reshape(B, T, GLA_VAL).astype(gv.dtype)
    return o * jax.nn.silu(gr), s.astype(s0.dtype)


def _alibi_slopes():
    h = jnp.arange(1, SWA_HEADS + 1, dtype=jnp.float32)
    return jnp.exp2(-8.0 * h / SWA_HEADS).reshape(SWA_KV_HEADS, SWA_GROUP)


def _sink_attention(q, k, v, dist, valid, sinks):
    f32 = jnp.float32
    s = jnp.einsum('...qkgd,...skd->...kgqs', q.astype(f32), k.astype(f32)) * (SWA_HEAD_DIM ** -0.5)
    s = s - _alibi_slopes()[:, :, None, None] * dist
    s = jnp.where(valid, s, -jnp.inf)
    sink = jnp.broadcast_to(sinks.astype(f32).reshape(SWA_KV_HEADS, SWA_GROUP, 1, 1), s.shape[:-1] + (1,))
    p = jax.nn.softmax(jnp.concatenate([s, sink], axis=-1), axis=-1)[..., :-1]
    o = jnp.einsum('...kgqs,...skd->...qkgd', p, v.astype(f32))
    return o.astype(q.dtype)


def _swa_prompt(q, k, v, sinks):
    B, T, _ = q.shape
    nb = T // WINDOW
    qb = q.reshape(B, nb, WINDOW, SWA_KV_HEADS, SWA_GROUP, SWA_HEAD_DIM)

    def band(a):
        ab = a.reshape(B, nb, WINDOW, SWA_KV_HEADS, SWA_HEAD_DIM)
        prev = jnp.concatenate([jnp.zeros_like(ab[:, :1]), ab[:, :-1]], axis=1)
        return jnp.concatenate([prev, ab], axis=2)

    i = jnp.arange(WINDOW)[:, None]
    j = jnp.arange(2 * WINDOW)[None, :]
    dist = i + WINDOW - j
    in_win = (dist >= 0) & (dist < WINDOW)
    has_key = (jnp.arange(nb) > 0)[:, None, None] | (j >= WINDOW)[None]
    valid = (in_win[None] & has_key)[:, None, None]
    o = _sink_attention(qb, band(k), band(v), dist.astype(jnp.float32), valid, sinks)
    return o.reshape(B, T, SWA_Q)


def _swa_sample(q, k, v, ck, cv, sinks):
    B, T, _ = q.shape
    WB = ck.shape[1]
    kk = jnp.concatenate([ck.astype(k.dtype), k.reshape(B, T, SWA_KV_HEADS, SWA_HEAD_DIM)], axis=1)
    vv = jnp.concatenate([cv.astype(v.dtype), v.reshape(B, T, SWA_KV_HEADS, SWA_HEAD_DIM)], axis=1)
    qq = q.reshape(B, T, SWA_KV_HEADS, SWA_GROUP, SWA_HEAD_DIM)
    dist = jnp.arange(T)[:, None] + WB - jnp.arange(WB + T)[None, :]
    valid = (dist >= 0) & (dist < WINDOW)
    o = _sink_attention(qq, kk, vv, dist.astype(jnp.float32), valid, sinks)
    return o.reshape(B, T, SWA_Q), kk[:, T:], vv[:, T:]


def _pool_branch(u, prefix, w_pool, pool_scale):
    B, T, _ = u.shape
    P = prefix.shape[1]
    ext = jnp.concatenate([prefix.astype(u.dtype), u], axis=1)
    c = jnp.cumsum(ext.astype(jnp.float32), axis=1)
    c = jnp.concatenate([jnp.zeros_like(c[:, :1]), c], axis=1)
    c = c.reshape(B, P + T + 1, POOL_GROUPS, POOL_GROUP_DIM)
    hi = P + 1 + jnp.arange(T)
    means = []
    for gi, w in enumerate(POOL_WINDOWS):
        lo = jnp.maximum(hi - w, 0)
        cg = c[:, :, gi]
        means.append((cg[:, hi] - cg[:, lo]) / (hi - lo).astype(jnp.float32)[:, None])
    pooled = jnp.stack(means, axis=2)
    y = pooled - u.reshape(B, T, POOL_GROUPS, POOL_GROUP_DIM).astype(jnp.float32)
    y = jnp.einsum('btgc,gce->btge', y, w_pool.astype(jnp.float32)).reshape(B, T, POOL_DIM)
    y = (y * pool_scale.astype(jnp.float32)).astype(u.dtype)
    return y, ext[:, ext.shape[1] - (POOL_MAX_WIN - 1):]


def _conv_ffn(h, prefix, w_up, w_conv, b_conv, w_down):
    a, g = jnp.split(h @ w_up, 2, axis=-1)
    T = a.shape[1]
    ext = jnp.concatenate([prefix.astype(a.dtype), a], axis=1)
    acc = b_conv + ext[:, 0:T] * w_conv[0]
    for j in range(1, CONV_WIDTH):
        acc = acc + ext[:, j:j + T] * w_conv[j]
    out = (jax.nn.gelu(acc) * g) @ w_down
    return out, ext[:, ext.shape[1] - (CONV_WIDTH - 1):]


def _layer(x, p, lw, gla_s0, swa_cache, pool_prefix, conv_prefix):
    B, T, _ = x.shape
    h = _rmsnorm(x, lw['g_mix'])
    gq, gk, gv, gr, glr, sq, sk, sv, pu = _split_in(h @ lw['w_in'])
    ya, gla_s = _gla_branch(gq, gk, gv, gr, glr, lw['w_gla_a2'], lw['b_gla_a'], lw['g_gla_out'], gla_s0)
    if swa_cache is None:
        yb = _swa_prompt(sq, sk, sv, lw['swa_sinks'])
        wb = min(WINDOW, T)
        kb = sk.reshape(B, T, SWA_KV_HEADS, SWA_HEAD_DIM)[:, T - wb:]
        vb = sv.reshape(B, T, SWA_KV_HEADS, SWA_HEAD_DIM)[:, T - wb:]
    else:
        yb, kb, vb = _swa_sample(sq, sk, sv, swa_cache[0], swa_cache[1], lw['swa_sinks'])
    yc, pool_s = _pool_branch(pu, pool_prefix, lw['w_pool'], lw['pool_scale'])
    ga, gb, gc = jnp.split(jax.nn.sigmoid(h @ lw['w_gate']), N_BRANCH, axis=-1)
    m = ga * (ya @ lw['w_branch_gla']) + gb * (yb @ lw['w_branch_swa']) + gc * (yc @ lw['w_branch_pool'])
    x = x + m @ lw['w_out']
    f, conv_s = _conv_ffn(_rmsnorm(x, lw['g_ffn']), conv_prefix, lw['w_up'], lw['w_conv'], lw['b_conv'], lw['w_down'])
    x = x + f
    x = x + (p @ lw['w_ple']) * jax.nn.sigmoid(_rmsnorm(x, lw['g_ple']) @ lw['w_ple_gate'])
    return x, (gla_s, kb, vb, pool_s, conv_s)


def setup_inputs(seed: int = 0) -> dict:
    key = jax.random.key(seed)
    ks = iter(jax.random.split(key, 32))

    def nrm(shape, scale):
        return jax.random.normal(next(ks), shape, jnp.float32) * scale

    def gain(shape):
        return 1.0 + nrm(shape, 0.05)

    wb = min(WINDOW, PAST_LEN)
    return {
        'x_prompt': nrm((BATCH, SEQ, D_MODEL), 1.0),
        'x_sample': nrm((DEC_BATCH, DEC_SEQ, D_MODEL), 1.0),
        'state_gla': nrm((DEPTH, DEC_BATCH, GLA_HEADS, GLA_DK, GLA_DV), 1.0),
        'cache_swa_k': nrm((DEPTH, DEC_BATCH, wb, SWA_KV_HEADS, SWA_HEAD_DIM), 1.0),
        'cache_swa_v': nrm((DEPTH, DEC_BATCH, wb, SWA_KV_HEADS, SWA_HEAD_DIM), 1.0),
        'state_pool': nrm((DEPTH, DEC_BATCH, POOL_MAX_WIN - 1, POOL_DIM), 1.0),
        'state_ffn_conv': nrm((DEPTH, DEC_BATCH, CONV_WIDTH - 1, D_FF), 1.0),
        'p_prompt': nrm((DEPTH, BATCH, SEQ, PLE_DIM), 1.0),
        'p_sample': nrm((DEPTH, DEC_BATCH, DEC_SEQ, PLE_DIM), 1.0),
        'g_mix': gain((DEPTH, D_MODEL)),
        'w_in': nrm((DEPTH, D_MODEL, N_IN), D_MODEL ** -0.5),
        'w_gla_a2': nrm((DEPTH, GLA_RANK, GLA_KEY), GLA_RANK ** -0.5),
        'b_gla_a': nrm((DEPTH, GLA_KEY), 0.1),
        'g_gla_out': gain((DEPTH, GLA_HEADS, GLA_DV)),
        'swa_sinks': nrm((DEPTH, SWA_HEADS), 0.5),
        'w_pool': nrm((DEPTH, POOL_GROUPS, POOL_GROUP_DIM, POOL_GROUP_DIM), POOL_GROUP_DIM ** -0.5),
        'pool_scale': gain((DEPTH, POOL_DIM)),
        'w_branch_gla': nrm((DEPTH, GLA_VAL, D_MODEL), GLA_VAL ** -0.5),
        'w_branch_swa': nrm((DEPTH, SWA_Q, D_MODEL), SWA_Q ** -0.5),
        'w_branch_pool': nrm((DEPTH, POOL_DIM, D_MODEL), POOL_DIM ** -0.5),
        'w_gate': nrm((DEPTH, D_MODEL, N_BRANCH * D_MODEL), D_MODEL ** -0.5),
        'w_out': nrm((DEPTH, D_MODEL, D_MODEL), D_MODEL ** -0.5),
        'g_ffn': gain((DEPTH, D_MODEL)),
        'w_up': nrm((DEPTH, D_MODEL, 2 * D_FF), D_MODEL ** -0.5),
        'w_conv': nrm((DEPTH, CONV_WIDTH, D_FF), CONV_WIDTH ** -0.5),
        'b_conv': nrm((DEPTH, D_FF), 0.01),
        'w_down': nrm((DEPTH, D_FF, D_MODEL), D_FF ** -0.5),
        'g_ple': gain((DEPTH, D_MODEL)),
        'w_ple': nrm((DEPTH, PLE_DIM, D_MODEL), PLE_DIM ** -0.5),
        'w_ple_gate': nrm((DEPTH, D_MODEL, D_MODEL), D_MODEL ** -0.5),
        'g_final': gain((D_MODEL,)),
    }


def reference(x_prompt, x_sample, state_gla, cache_swa_k, cache_swa_v, state_pool, state_ffn_conv,
              p_prompt, p_sample, g_mix, w_in, w_gla_a2, b_gla_a, g_gla_out, swa_sinks, w_pool,
              pool_scale, w_branch_gla, w_branch_swa, w_branch_pool, w_gate, w_out, g_ffn, w_up,
              w_conv, b_conv, w_down, g_ple, w_ple, w_ple_gate, g_final):
    xp, xs = x_prompt, x_sample
    B = xp.shape[0]
    new_p, new_s = [], []
    for i in range(DEPTH):
        lw = dict(g_mix=g_mix[i], w_in=w_in[i], w_gla_a2=w_gla_a2[i], b_gla_a=b_gla_a[i],
                  g_gla_out=g_gla_out[i], swa_sinks=swa_sinks[i], w_pool=w_pool[i],
                  pool_scale=pool_scale[i], w_branch_gla=w_branch_gla[i], w_branch_swa=w_branch_swa[i],
                  w_branch_pool=w_branch_pool[i], w_gate=w_gate[i], w_out=w_out[i], g_ffn=g_ffn[i],
                  w_up=w_up[i], w_conv=w_conv[i], b_conv=b_conv[i], w_down=w_down[i], g_ple=g_ple[i],
                  w_ple=w_ple[i], w_ple_gate=w_ple_gate[i])
        xp, st_p = _layer(xp, p_prompt[i], lw,
                          jnp.zeros((B, GLA_HEADS, GLA_DK, GLA_DV), xp.dtype), None,
                          jnp.zeros((B, 0, POOL_DIM), xp.dtype),
                          jnp.zeros((B, CONV_WIDTH - 1, D_FF), xp.dtype))
        xs, st_s = _layer(xs, p_sample[i], lw, state_gla[i], (cache_swa_k[i], cache_swa_v[i]),
                          state_pool[i], state_ffn_conv[i])
        new_p.append(st_p)
        new_s.append(st_s)
    y_prompt = _rmsnorm(xp, g_final)
    y_sample = _rmsnorm(xs, g_final)
    gla_p = jnp.stack([e[0] for e in new_p])
    swa_k_p = jnp.stack([e[1] for e in new_p])
    swa_v_p = jnp.stack([e[2] for e in new_p])
    pool_p = jnp.stack([e[3] for e in new_p])
    conv_p = jnp.stack([e[4] for e in new_p])
    gla_s = jnp.stack([e[0] for e in new_s])
    swa_k_s = jnp.stack([e[1] for e in new_s])
    swa_v_s = jnp.stack([e[2] for e in new_s])
    pool_s = jnp.stack([e[3] for e in new_s])
    conv_s = jnp.stack([e[4] for e in new_s])
    return (y_prompt, y_sample, gla_p, swa_k_p, swa_v_p, pool_p, conv_p, gla_s, swa_k_s, swa_v_s, pool_s, conv_s)
```

```python
import functools
import math

import jax
import jax.numpy as jnp
from jax import lax
from jax.experimental import pallas as pl
from jax.experimental.pallas import tpu as pltpu

EPS = 1e-6
GLA_TAU = 16.0
GLA_CHUNK = 64
WINDOW = 128
POOL_WINDOWS = (2, 4, 8, 16)

VMEM_LIMIT_BYTES = 56 * 1024 * 1024
LANES = 128
BF16_SUBLANES = 16
POOL_HALO = 16
CONV_HALO = 8

F32 = jnp.float32
BF16 = jnp.bfloat16


def _pick_tile(n, target, align):
    best = None
    for t in range(align, min(n, target) + 1, align):
        if n % t == 0:
            best = t
    if best is None:
        return n
    return best


def _round_up(n, m):
    return (n + m - 1) // m * m


def _params(*sem):
    return pltpu.CompilerParams(dimension_semantics=sem, vmem_limit_bytes=VMEM_LIMIT_BYTES)


def _rmsnorm_kernel(x_ref, g_ref, o_ref):
    x = x_ref[...]
    y = x * lax.rsqrt(jnp.mean(x * x, axis=-1, keepdims=True) + EPS)
    o_ref[...] = (y * g_ref[...]).astype(o_ref.dtype)


def _rmsnorm(x, g, out_dtype):
    M, D = x.shape
    tm = _pick_tile(M, 512, BF16_SUBLANES)
    return pl.pallas_call(
        _rmsnorm_kernel,
        grid=(M // tm,),
        in_specs=[pl.BlockSpec((tm, D), lambda i: (i, 0)),
                  pl.BlockSpec((1, D), lambda i: (0, 0))],
        out_specs=pl.BlockSpec((tm, D), lambda i: (i, 0)),
        out_shape=jax.ShapeDtypeStruct((M, D), out_dtype),
        compiler_params=_params("parallel"),
        name="rmsnorm",
    )(x, g.reshape(1, D))


def _mm_kernel(*refs, n_acts, pair_act, n_extra, combine):
    n_pairs = len(pair_act)
    w_refs = refs[n_acts:n_acts + n_pairs]
    extras = refs[n_acts + n_pairs:n_acts + n_pairs + n_extra]
    o_ref = refs[n_acts + n_pairs + n_extra]

    def dot(p):
        return jnp.dot(refs[pair_act[p]][...], w_refs[p][...], preferred_element_type=F32)

    o_ref[...] = combine(dot, extras).astype(o_ref.dtype)


def _fused_matmul(acts, pairs, extras, combine, n_out, out_dtype, tm, tn, name):
    M = acts[0].shape[0]
    in_specs = [pl.BlockSpec((tm, a.shape[1]), lambda i, j: (i, 0)) for a in acts]
    args = list(acts)
    for ai, w, off in pairs:
        K = acts[ai].shape[1]
        assert off % tn == 0 and w.shape[0] == K
        in_specs.append(pl.BlockSpec((K, tn), lambda i, j, ob=off // tn: (0, j + ob)))
        args.append(w)
    for e in extras:
        in_specs.append(pl.BlockSpec((tm, tn), lambda i, j: (i, j)))
        args.append(e)
    return pl.pallas_call(
        functools.partial(_mm_kernel, n_acts=len(acts), pair_act=tuple(p[0] for p in pairs),
                          n_extra=len(extras), combine=combine),
        grid=(M // tm, n_out // tn),
        in_specs=in_specs,
        out_specs=pl.BlockSpec((tm, tn), lambda i, j: (i, j)),
        out_shape=jax.ShapeDtypeStruct((M, n_out), out_dtype),
        compiler_params=_params("parallel", "parallel"),
        name=name,
    )(*args)


def _mm_ksplit_kernel(a_ref, w_ref, x_ref, o_ref, acc_ref, *, nk):
    k = pl.program_id(2)
    d = jnp.dot(a_ref[...], w_ref[...], preferred_element_type=F32)

    @pl.when(k == 0)
    def _():
        acc_ref[...] = d

    @pl.when(k > 0)
    def _():
        acc_ref[...] += d

    @pl.when(k == nk - 1)
    def _():
        o_ref[...] = x_ref[...] + acc_ref[...]


def _residual_matmul_ksplit(a, w, x, tm, tn, tk, name):
    M, K = a.shape
    N = w.shape[1]
    nk = K // tk
    return pl.pallas_call(
        functools.partial(_mm_ksplit_kernel, nk=nk),
        grid=(M // tm, N // tn, nk),
        in_specs=[pl.BlockSpec((tm, tk), lambda i, j, k: (i, k)),
                  pl.BlockSpec((tk, tn), lambda i, j, k: (k, j)),
                  pl.BlockSpec((tm, tn), lambda i, j, k: (i, j))],
        out_specs=pl.BlockSpec((tm, tn), lambda i, j, k: (i, j)),
        out_shape=jax.ShapeDtypeStruct((M, N), F32),
        scratch_shapes=[pltpu.VMEM((tm, tn), F32)],
        compiler_params=_params("parallel", "parallel", "arbitrary"),
        name=name,
    )(a, w, x)


def _gla_chunk(q, k, v, glr, w_a2, b_a, S, C):
    DK = q.shape[1]
    la = jnp.dot(glr.astype(BF16), w_a2, preferred_element_type=F32) + b_a
    log_a = (jnp.minimum(la, 0.0) - jnp.log1p(jnp.exp(-jnp.abs(la)))) / GLA_TAU
    row = lax.broadcasted_iota(jnp.int32, (C, DK), 0)
    b = log_a
    s = 1
    while s < C:
        b = b + jnp.where(row >= s, pltpu.roll(b, s, 0), 0.0)
        s *= 2
    b_last = b[C - 1:C, :]
    q_t = ((q * (DK ** -0.5)) * jnp.exp(b)).astype(BF16)
    k_t = (k * jnp.exp(-b)).astype(BF16)
    att = lax.dot_general(q_t, k_t, (((1,), (1,)), ((), ())), preferred_element_type=F32)
    ri = lax.broadcasted_iota(jnp.int32, (C, C), 0)
    ci = lax.broadcasted_iota(jnp.int32, (C, C), 1)
    att = jnp.where(ri >= ci, att, 0.0)
    vb = v.astype(BF16)
    o = (jnp.dot(att.astype(BF16), vb, preferred_element_type=F32)
         + jnp.dot(q_t, S.astype(BF16), preferred_element_type=F32))
    k_end = (k * jnp.exp(b_last - b)).astype(BF16)
    er = lax.broadcasted_iota(jnp.int32, (DK, DK), 0)
    ec = lax.broadcasted_iota(jnp.int32, (DK, DK), 1)
    decay_col = jnp.sum(jnp.where(er == ec, jnp.exp(b_last), 0.0), axis=1, keepdims=True)
    S_new = decay_col * S + lax.dot_general(k_end, vb, (((0,), (0,)), ((), ())),
                                            preferred_element_type=F32)
    return o, S_new


def _gla_kernel(q_ref, k_ref, v_ref, r_ref, lr_ref, wa_ref, ba_ref, go_ref, s0_ref,
                y_ref, sout_ref, s_scr, *, C, nb, rank, n_chunks):
    n = pl.program_id(2)

    @pl.when(n == 0)
    def _():
        s_scr[...] = s0_ref[...]

    outs = []
    for j in range(nb):
        rows = slice(j * C, (j + 1) * C)
        o, s_new = _gla_chunk(q_ref[rows, :], k_ref[rows, :], v_ref[rows, :], lr_ref[rows, :rank],
                              wa_ref[...], ba_ref[...], s_scr[j], C)
        s_scr[j] = s_new
        o = o * lax.rsqrt(jnp.mean(o * o, axis=-1, keepdims=True) + EPS) * go_ref[...]
        r = r_ref[rows, :]
        outs.append(o * (r * jax.nn.sigmoid(r)))
    y = outs[0] if nb == 1 else jnp.concatenate(outs, axis=0)
    y_ref[...] = y.astype(y_ref.dtype)

    @pl.when(n == n_chunks - 1)
    def _():
        sout_ref[...] = s_scr[...]


def _gla(z, zlr, w_a2, b_a, g_out, s0, row0, T, off_q, off_k, off_v, off_r):
    B, H, DK, DV = s0.shape
    rank = w_a2.shape[0]
    C = math.gcd(T, GLA_CHUNK)
    n_chunks = T // C
    nb = 1 if C % BF16_SUBLANES == 0 else BF16_SUBLANES // C
    assert nb == 1 or n_chunks == 1
    assert B % nb == 0 and row0 % (nb * C) == 0
    R = nb * C
    rb0 = row0 // R

    def rows(b, h, n):
        return rb0 + b * n_chunks + n

    def zspec(width, off):
        assert off % width == 0
        return pl.BlockSpec((R, width), lambda b, h, n: (rows(b, h, n), off // width + h))

    return pl.pallas_call(
        functools.partial(_gla_kernel, C=C, nb=nb, rank=rank, n_chunks=n_chunks),
        grid=(B // nb, H, n_chunks),
        in_specs=[zspec(DK, off_q), zspec(DK, off_k), zspec(DV, off_v), zspec(DV, off_r),
                  pl.BlockSpec((R, zlr.shape[1]), lambda b, h, n: (rows(b, h, n), 0)),
                  pl.BlockSpec((rank, DK), lambda b, h, n: (0, h)),
                  pl.BlockSpec((1, DK), lambda b, h, n: (0, h)),
                  pl.BlockSpec((1, DV), lambda b, h, n: (0, h)),
                  pl.BlockSpec((nb, None, DK, DV), lambda b, h, n: (b, h, 0, 0))],
        out_specs=[pl.BlockSpec((R, DV), lambda b, h, n: (b * n_chunks + n, h)),
                   pl.BlockSpec((nb, None, DK, DV), lambda b, h, n: (b, h, 0, 0))],
        out_shape=[jax.ShapeDtypeStruct((B * T, H * DV), BF16),
                   jax.ShapeDtypeStruct((B, H, DK, DV), F32)],
        scratch_shapes=[pltpu.VMEM((nb, DK, DV), F32)],
        compiler_params=_params("parallel", "parallel", "arbitrary"),
        name="gla",
    )(z, z, z, z, zlr, w_a2, b_a, g_out, s0)


def _swa_kernel(q_ref, k_ref, v_ref, kp_ref, vp_ref, slope_ref, sink_ref, y_ref,
                *, Tq, nb, KV, G, dh, first_block_has_no_past):
    n = pl.program_id(1)
    W = WINDOW
    rows, keys = G * Tq, W + Tq
    ri = lax.broadcasted_iota(jnp.int32, (rows, keys), 0)
    ci = lax.broadcasted_iota(jnp.int32, (rows, keys), 1)
    dist = (ri % Tq) + W - ci
    valid = (dist >= 0) & (dist < WINDOW)
    if first_block_has_no_past:
        valid = valid & ((ci >= W) | (n > 0))
    distf = dist.astype(F32)
    per_seq = []
    for j in range(nb):
        rs = slice(j * Tq, (j + 1) * Tq)
        q, k_own, v_own = q_ref[rs, :], k_ref[rs, :], v_ref[rs, :]
        k_past = kp_ref[j] if len(kp_ref.shape) == 3 else kp_ref[...]
        v_past = vp_ref[j] if len(vp_ref.shape) == 3 else vp_ref[...]
        heads = []
        for kv in range(KV):
            cs = slice(kv * dh, (kv + 1) * dh)
            kk = jnp.concatenate([k_past[:, cs], k_own[:, cs]], axis=0).astype(BF16)
            vv = jnp.concatenate([v_past[:, cs], v_own[:, cs]], axis=0).astype(BF16)
            qs = jnp.concatenate(
                [q[:, (kv * G + g) * dh:(kv * G + g + 1) * dh] for g in range(G)], axis=0).astype(BF16)
            s = lax.dot_general(qs, kk, (((1,), (1,)), ((), ())), preferred_element_type=F32)
            s = s * (dh ** -0.5) - slope_ref[kv] * distf
            s = jnp.where(valid, s, -jnp.inf)
            sink = sink_ref[kv]
            m = jnp.maximum(jnp.max(s, axis=1, keepdims=True), sink)
            e = jnp.exp(s - m)
            denom = jnp.sum(e, axis=1, keepdims=True) + jnp.exp(sink - m)
            o = jnp.dot((e / denom).astype(BF16), vv, preferred_element_type=F32)
            heads += [o[g * Tq:(g + 1) * Tq, :] for g in range(G)]
        per_seq.append(jnp.concatenate(heads, axis=1))
    y = per_seq[0] if nb == 1 else jnp.concatenate(per_seq, axis=0)
    y_ref[...] = y.astype(y_ref.dtype)


def _swa(z, k_past, v_past, slope_col, sink_col, row0, B, T, Tq, KV, G, dh, off_q, off_k, off_v):
    nq = T // Tq
    nb = 1 if Tq % BF16_SUBLANES == 0 else BF16_SUBLANES // Tq
    assert nb == 1 or nq == 1
    R = nb * Tq
    assert B % nb == 0 and row0 % R == 0
    rb0 = row0 // R
    QW, KW = KV * G * dh, KV * dh
    assert off_q % QW == 0 and off_k % KW == 0 and off_v % KW == 0

    def own(width, off):
        return pl.BlockSpec((R, width), lambda b, n: (rb0 + b * nq + n, off // width))

    if k_past is None:
        assert Tq == WINDOW and nb == 1

        def past(off):
            return pl.BlockSpec((WINDOW, KW), lambda b, n: (jnp.maximum(rb0 + b * nq + n - 1, 0), off // KW))

        past_specs, past_args = [past(off_k), past(off_v)], [z, z]
    else:
        spec = pl.BlockSpec((nb, WINDOW, KW), lambda b, n: (b, 0, 0))
        past_specs, past_args = [spec, spec], [k_past, v_past]

    const = pl.BlockSpec((KV, G * Tq, 1), lambda b, n: (0, 0, 0))
    return pl.pallas_call(
        functools.partial(_swa_kernel, Tq=Tq, nb=nb, KV=KV, G=G, dh=dh,
                          first_block_has_no_past=k_past is None),
        grid=(B // nb, nq),
        in_specs=[own(QW, off_q), own(KW, off_k), own(KW, off_v)] + past_specs + [const, const],
        out_specs=pl.BlockSpec((R, QW), lambda b, n: (b * nq + n, 0)),
        out_shape=jax.ShapeDtypeStruct((B * T, QW), BF16),
        compiler_params=_params("parallel", "arbitrary"),
        name="swa",
    )(z, z, z, *past_args, slope_col, sink_col)


def _pool_kernel(u_ref, halo_ref, w_ref, sc_ref, y_ref, *, Tt, nb, P, G, Cg, first_block_has_no_past):
    n = pl.program_id(1)
    t_glob = n * Tt + lax.broadcasted_iota(jnp.int32, (Tt, 1), 0)
    per_seq = []
    for j in range(nb):
        u = u_ref[j * Tt:(j + 1) * Tt, :]
        halo = halo_ref[j] if len(halo_ref.shape) == 3 else halo_ref[...]
        if first_block_has_no_past:
            halo = jnp.where(n > 0, halo, 0.0)
        ext = jnp.concatenate([halo, u], axis=0)
        groups = []
        for g, w in enumerate(POOL_WINDOWS):
            cs = slice(g * Cg, (g + 1) * Cg)
            s = ext[:, cs]
            d = 1
            while d < w:
                s = s + pltpu.roll(s, d, 0)
                d *= 2
            count = jnp.minimum(w, P + 1 + t_glob).astype(F32)
            y = s[POOL_HALO:, :] / count - u[:, cs]
            groups.append(jnp.dot(y.astype(BF16), w_ref[g], preferred_element_type=F32))
        per_seq.append(jnp.concatenate(groups, axis=1) * sc_ref[...])
    y = per_seq[0] if nb == 1 else jnp.concatenate(per_seq, axis=0)
    y_ref[...] = y.astype(y_ref.dtype)


def _pool(z, prefix, w_pool, pool_scale, row0, B, T, Tt, off_u):
    G, Cg, _ = w_pool.shape
    PD = G * Cg
    nt = T // Tt
    nb = 1 if Tt % BF16_SUBLANES == 0 else BF16_SUBLANES // Tt
    assert nb == 1 or nt == 1
    R = nb * Tt
    assert B % nb == 0 and row0 % R == 0 and off_u % PD == 0
    assert Tt % POOL_HALO == 0 or prefix is not None
    rb0 = row0 // R
    if prefix is None:
        hb = Tt // POOL_HALO
        halo_spec = pl.BlockSpec((POOL_HALO, PD),
                                 lambda b, n: (jnp.maximum((rb0 + b * nt + n) * hb - 1, 0), off_u // PD))
        halo_arg, P = z, 0
    else:
        halo_spec = pl.BlockSpec((nb, POOL_HALO, PD), lambda b, n: (b, 0, 0))
        halo_arg, P = prefix, POOL_HALO - 1
    return pl.pallas_call(
        functools.partial(_pool_kernel, Tt=Tt, nb=nb, P=P, G=G, Cg=Cg,
                          first_block_has_no_past=prefix is None),
        grid=(B // nb, nt),
        in_specs=[pl.BlockSpec((R, PD), lambda b, n: (rb0 + b * nt + n, off_u // PD)),
                  halo_spec,
                  pl.BlockSpec((G, Cg, Cg), lambda b, n: (0, 0, 0)),
                  pl.BlockSpec((1, PD), lambda b, n: (0, 0))],
        out_specs=pl.BlockSpec((R, PD), lambda b, n: (b * nt + n, 0)),
        out_shape=jax.ShapeDtypeStruct((B * T, PD), BF16),
        compiler_params=_params("parallel", "arbitrary"),
        name="pool",
    )(z, halo_arg, w_pool, pool_scale)


def _convact_kernel(a_ref, g_ref, halo_ref, wc_ref, bc_ref, o_ref, *, Tt, nb, first_block_has_no_past):
    n = pl.program_id(1)
    w0, w1, w2 = wc_ref[0:1, :], wc_ref[1:2, :], wc_ref[2:3, :]
    per_seq = []
    for j in range(nb):
        rs = slice(j * Tt, (j + 1) * Tt)
        a = a_ref[rs, :]
        halo = halo_ref[j] if len(halo_ref.shape) == 3 else halo_ref[...]
        if first_block_has_no_past:
            halo = jnp.where(n > 0, halo, 0.0)
        ext = jnp.concatenate([halo, a], axis=0)
        a1 = pltpu.roll(ext, 1, 0)[CONV_HALO:, :]
        a2 = pltpu.roll(ext, 2, 0)[CONV_HALO:, :]
        acc = bc_ref[...] + a2 * w0
        acc = acc + a1 * w1
        acc = acc + a * w2
        per_seq.append(jax.nn.gelu(acc) * g_ref[rs, :])
    y = per_seq[0] if nb == 1 else jnp.concatenate(per_seq, axis=0)
    o_ref[...] = y.astype(o_ref.dtype)


def _convact(up, prefix, w_conv, b_conv, row0, B, T, Tt, tc):
    Fp = up.shape[1] // 2
    nt, nf = T // Tt, Fp // tc
    nb = 1 if Tt % BF16_SUBLANES == 0 else BF16_SUBLANES // Tt
    assert nb == 1 or nt == 1
    R = nb * Tt
    assert B % nb == 0 and row0 % R == 0 and Tt % CONV_HALO == 0
    rb0 = row0 // R
    if prefix is None:
        hb = Tt // CONV_HALO
        halo_spec = pl.BlockSpec((CONV_HALO, tc),
                                 lambda b, n, f: (jnp.maximum((rb0 + b * nt + n) * hb - 1, 0), f))
        halo_arg = up
    else:
        halo_spec = pl.BlockSpec((nb, CONV_HALO, tc), lambda b, n, f: (b, 0, f))
        halo_arg = prefix
    return pl.pallas_call(
        functools.partial(_convact_kernel, Tt=Tt, nb=nb, first_block_has_no_past=prefix is None),
        grid=(B // nb, nt, nf),
        in_specs=[pl.BlockSpec((R, tc), lambda b, n, f: (rb0 + b * nt + n, f)),
                  pl.BlockSpec((R, tc), lambda b, n, f: (rb0 + b * nt + n, nf + f)),
                  halo_spec,
                  pl.BlockSpec((w_conv.shape[0], tc), lambda b, n, f: (0, f)),
                  pl.BlockSpec((1, tc), lambda b, n, f: (0, f))],
        out_specs=pl.BlockSpec((R, tc), lambda b, n, f: (b * nt + n, f)),
        out_shape=jax.ShapeDtypeStruct((B * T, Fp), BF16),
        compiler_params=_params("parallel", "arbitrary", "arbitrary"),
        name="convact",
    )(up, up, halo_arg, w_conv, b_conv)


def _plain(dot, extras):
    return dot(0)


def _gate_combine(dot, extras):
    acc = jax.nn.sigmoid(dot(0)) * dot(1)
    acc = acc + jax.nn.sigmoid(dot(2)) * dot(3)
    return acc + jax.nn.sigmoid(dot(4)) * dot(5)


def _residual(dot, extras):
    return extras[0][...] + dot(0)


def _ple(dot, extras):
    return extras[0][...] + dot(1) * jax.nn.sigmoid(dot(0))


def kernel(x_prompt, x_sample, state_gla, cache_swa_k, cache_swa_v, state_pool, state_ffn_conv, p_prompt, p_sample, g_mix, w_in, w_gla_a2, b_gla_a, g_gla_out, swa_sinks, w_pool, pool_scale, w_branch_gla, w_branch_swa, w_branch_pool, w_gate, w_out, g_ffn, w_up, w_conv, b_conv, w_down, g_ple, w_ple, w_ple_gate, g_final):
    Bp, Tp, D = x_prompt.shape
    Bs, Ts, _ = x_sample.shape
    depth = w_in.shape[0]
    _, _, H, DK, DV = state_gla.shape
    rank = w_gla_a2.shape[1]
    _, _, WB, KV, dh = cache_swa_k.shape
    NH = swa_sinks.shape[1]
    G = NH // KV
    _, PG, Cg, _ = w_pool.shape
    PD = PG * Cg
    P = state_pool.shape[2]
    F = w_down.shape[1]
    CW = w_conv.shape[1]
    assert WB == WINDOW and P == POOL_HALO - 1 and CW == 3 and PG == len(POOL_WINDOWS)
    GK, GV, SQ, SKV = H * DK, H * DV, NH * dh, KV * dh
    Mp, Ms = Bp * Tp, Bs * Ts
    M = Mp + Ms

    off_q, off_k, off_v, off_r = 0, GK, 2 * GK, 2 * GK + GV
    off_u = 2 * GK + 2 * GV
    off_sq = off_u + PD
    off_sk = off_sq + SQ
    off_sv = off_sk + SKV
    NZ = off_sv + SKV
    in_sizes = (GK, GK, GV, GV, rank, SQ, SKV, SKV, PD)
    in_off = [0]
    for s in in_sizes:
        in_off.append(in_off[-1] + s)

    tm = _pick_tile(M, 1056, BF16_SUBLANES)
    tm_half = _pick_tile(M, 528, BF16_SUBLANES)
    Fp = _round_up(F, 512)
    tf = 512
    tk_down = _pick_tile(Fp, 2816, LANES)
    tn_z = _pick_tile(NZ, 768, LANES)
    tn_d = _pick_tile(D, 512, LANES)
    tn_g = _pick_tile(D, 256, LANES)
    Tt_pool = _pick_tile(Tp, 256, POOL_HALO)
    Tt_conv = _pick_tile(Tp, 512, BF16_SUBLANES)

    heads = jnp.arange(1, NH + 1, dtype=F32)
    slopes = jnp.exp2(-8.0 * heads / NH).reshape(KV, G, 1)

    def head_cols(v, tq):
        return jnp.broadcast_to(v[:, :, None, :], (KV, G, tq, 1)).reshape(KV, G * tq, 1)

    x = jnp.concatenate([x_prompt.reshape(Mp, D), x_sample.reshape(Ms, D)], axis=0)
    zero_state = jnp.zeros((Bp, H, DK, DV), F32)
    outs_p, outs_s = [], []

    for i in range(depth):
        wi = w_in[i]
        cols = [wi[:, in_off[c]:in_off[c + 1]] for c in range(9)]
        w_main = jnp.concatenate([cols[0], cols[1], cols[2], cols[3], cols[8], cols[5], cols[6], cols[7]],
                                 axis=1).astype(BF16)
        w_lr = jnp.pad(cols[4], ((0, 0), (0, LANES - rank))).astype(BF16)
        wu = w_up[i]
        w_up_p = jnp.concatenate([jnp.pad(wu[:, :F], ((0, 0), (0, Fp - F))),
                                  jnp.pad(wu[:, F:], ((0, 0), (0, Fp - F)))], axis=1).astype(BF16)
        w_down_p = jnp.pad(w_down[i], ((0, Fp - F), (0, 0))).astype(BF16)
        w_conv_p = jnp.pad(w_conv[i], ((0, 0), (0, Fp - F)))
        b_conv_p = jnp.pad(b_conv[i], (0, Fp - F)).reshape(1, Fp)
        w_gate_b = w_gate[i].astype(BF16)
        w_br = [w_branch_gla[i].astype(BF16), w_branch_swa[i].astype(BF16), w_branch_pool[i].astype(BF16)]
        w_out_b = w_out[i].astype(BF16)
        w_ple_b = w_ple[i].astype(BF16)
        w_pg_b = w_ple_gate[i].astype(BF16)
        w_a2_b = w_gla_a2[i].astype(BF16)
        w_pool_b = w_pool[i].astype(BF16)
        b_a = b_gla_a[i].reshape(1, GK)
        g_o = g_gla_out[i].reshape(1, GV)
        p_scale = pool_scale[i].reshape(1, PD)
        sinks = swa_sinks[i].reshape(KV, G, 1)

        h = _rmsnorm(x, g_mix[i], BF16)
        z = _fused_matmul([h], [(0, w_main, 0)], [], _plain, NZ, F32, tm, tn_z, "in_proj")
        zlr = _fused_matmul([h], [(0, w_lr, 0)], [], _plain, LANES, F32, tm, LANES, "in_proj_lr")

        gla_args = (w_a2_b, b_a, g_o)
        ya_p, gla_p = _gla(z, zlr, *gla_args, zero_state, 0, Tp, off_q, off_k, off_v, off_r)
        ya_s, gla_s = _gla(z, zlr, *gla_args, state_gla[i], Mp, Ts, off_q, off_k, off_v, off_r)

        swa_dims = (KV, G, dh, off_sq, off_sk, off_sv)
        yb_p = _swa(z, None, None, head_cols(slopes, WINDOW), head_cols(sinks, WINDOW),
                    0, Bp, Tp, WINDOW, *swa_dims)
        ck = cache_swa_k[i].reshape(Bs, WB, SKV)
        cv = cache_swa_v[i].reshape(Bs, WB, SKV)
        yb_s = _swa(z, ck, cv, head_cols(slopes, Ts), head_cols(sinks, Ts), Mp, Bs, Ts, Ts, *swa_dims)

        yc_p = _pool(z, None, w_pool_b, p_scale, 0, Bp, Tp, Tt_pool, off_u)
        pool_prefix = jnp.pad(state_pool[i], ((0, 0), (POOL_HALO - P, 0), (0, 0)))
        yc_s = _pool(z, pool_prefix, w_pool_b, p_scale, Mp, Bs, Ts, Ts, off_u)

        ya = jnp.concatenate([ya_p, ya_s], axis=0)
        yb = jnp.concatenate([yb_p, yb_s], axis=0)
        yc = jnp.concatenate([yc_p, yc_s], axis=0)

        m = _fused_matmul([h, ya, yb, yc],
                          [(0, w_gate_b, 0), (1, w_br[0], 0), (0, w_gate_b, D), (2, w_br[1], 0),
                           (0, w_gate_b, 2 * D), (3, w_br[2], 0)],
                          [], _gate_combine, D, BF16, tm_half, tn_g, "gate_combine")
        x = _fused_matmul([m], [(0, w_out_b, 0)], [x], _residual, D, F32, tm, tn_d, "out_proj")

        h = _rmsnorm(x, g_ffn[i], BF16)
        up = _fused_matmul([h], [(0, w_up_p, 0)], [], _plain, 2 * Fp, F32, tm, tf, "up_proj")
        act_p = _convact(up, None, w_conv_p, b_conv_p, 0, Bp, Tp, Tt_conv, tf)
        conv_prefix = jnp.pad(state_ffn_conv[i], ((0, 0), (CONV_HALO - (CW - 1), 0), (0, Fp - F)))
        act_s = _convact(up, conv_prefix, w_conv_p, b_conv_p, Mp, Bs, Ts, Ts, tf)
        act = jnp.concatenate([act_p, act_s], axis=0)
        x = _residual_matmul_ksplit(act, w_down_p, x, tm, tn_d, tk_down, "down_proj")

        h = _rmsnorm(x, g_ple[i], BF16)
        p = jnp.concatenate([p_prompt[i].reshape(Mp, -1), p_sample[i].reshape(Ms, -1)], axis=0).astype(BF16)
        x = _fused_matmul([h, p], [(0, w_pg_b, 0), (1, w_ple_b, 0)], [x], _ple, D, F32, tm, tn_d, "ple")

        zp = z[:Mp].reshape(Bp, Tp, NZ)
        zs = z[Mp:].reshape(Bs, Ts, NZ)
        wb = min(WINDOW, Tp)
        swa_k_p = zp[:, Tp - wb:, off_sk:off_sk + SKV].reshape(Bp, wb, KV, dh)
        swa_v_p = zp[:, Tp - wb:, off_sv:off_sv + SKV].reshape(Bp, wb, KV, dh)
        pool_p = zp[:, Tp - P:, off_u:off_u + PD]
        a_p = up[:Mp, :F].reshape(Bp, Tp, F)
        conv_p = a_p[:, Tp - (CW - 1):]
        swa_k_s = jnp.concatenate([ck, zs[:, :, off_sk:off_sk + SKV]], axis=1)[:, Ts:].reshape(Bs, WB, KV, dh)
        swa_v_s = jnp.concatenate([cv, zs[:, :, off_sv:off_sv + SKV]], axis=1)[:, Ts:].reshape(Bs, WB, KV, dh)
        pool_s = jnp.concatenate([state_pool[i], zs[:, :, off_u:off_u + PD]], axis=1)[:, Ts:]
        a_s = up[Mp:, :F].reshape(Bs, Ts, F)
        conv_s = jnp.concatenate([state_ffn_conv[i], a_s], axis=1)[:, Ts:]
        outs_p.append((gla_p, swa_k_p, swa_v_p, pool_p, conv_p))
        outs_s.append((gla_s, swa_k_s, swa_v_s, pool_s, conv_s))

    y = _rmsnorm(x, g_final, F32)
    y_prompt = y[:Mp].reshape(Bp, Tp, D)
    y_sample = y[Mp:].reshape(Bs, Ts, D)
    stack = lambda outs, c: jnp.stack([o[c] for o in outs])
    return (y_prompt, y_sample,
            stack(outs_p, 0), stack(outs_p, 1), stack(outs_p, 2), stack(outs_p, 3), stack(outs_p, 4),
            stack(outs_s, 0), stack(outs_s, 1), stack(outs_s, 2), stack(outs_s, 3), stack(outs_s, 4))
```

```python
import functools
import math

import jax
import jax.numpy as jnp
from jax import lax
from jax.experimental import pallas as pl
from jax.experimental.pallas import tpu as pltpu

EPS = 1e-6
GLA_TAU = 16.0
GLA_CHUNK = 64
WINDOW = 128
POOL_WINDOWS = (2, 4, 8, 16)

VMEM_LIMIT_BYTES = 56 * 1024 * 1024
LANES = 128
SUBLANES = 8
BF16_SUBLANES = 16
POOL_HALO = 16
CAST_BLOCK_BYTES = 6 * 1024 * 1024

F32 = jnp.float32
BF16 = jnp.bfloat16


def _pick_tile(n, target, align):
    best = None
    for t in range(align, min(n, target) + 1, align):
        if n % t == 0:
            best = t
    return n if best is None else best


def _params(*sem):
    return pltpu.CompilerParams(dimension_semantics=sem, vmem_limit_bytes=VMEM_LIMIT_BYTES)


def _cast_kernel(w_ref, o_ref):
    o_ref[...] = w_ref[...].astype(o_ref.dtype)


def _cast_weights(w):
    L, K, N = w.shape
    tk = _pick_tile(K, max(BF16_SUBLANES, CAST_BLOCK_BYTES // (4 * N)), BF16_SUBLANES)
    spec = pl.BlockSpec((None, tk, N), lambda l, k: (l, k, 0))
    return pl.pallas_call(
        _cast_kernel, grid=(L, K // tk), in_specs=[spec], out_specs=spec,
        out_shape=jax.ShapeDtypeStruct((L, K, N), BF16),
        compiler_params=_params("parallel", "parallel"), name="cast_weights",
    )(w)


def _cast_in_proj_kernel(w_ref, main_ref, lr_ref, *, segments, lr):
    w = w_ref[...]
    main_ref[...] = jnp.concatenate([w[:, a:b] for a, b in segments], axis=1).astype(main_ref.dtype)
    pad = jnp.zeros((w.shape[0], lr_ref.shape[1] - (lr[1] - lr[0])), F32)
    lr_ref[...] = jnp.concatenate([w[:, lr[0]:lr[1]], pad], axis=1).astype(lr_ref.dtype)


def _cast_in_proj(w, segments, lr):
    L, K, N = w.shape
    n_main = sum(b - a for a, b in segments)
    tk = _pick_tile(K, max(BF16_SUBLANES, CAST_BLOCK_BYTES // (8 * N)), BF16_SUBLANES)
    return pl.pallas_call(
        functools.partial(_cast_in_proj_kernel, segments=segments, lr=lr),
        grid=(L, K // tk),
        in_specs=[pl.BlockSpec((None, tk, N), lambda l, k: (l, k, 0))],
        out_specs=[pl.BlockSpec((None, tk, n_main), lambda l, k: (l, k, 0)),
                   pl.BlockSpec((None, tk, LANES), lambda l, k: (l, k, 0))],
        out_shape=[jax.ShapeDtypeStruct((L, K, n_main), BF16),
                   jax.ShapeDtypeStruct((L, K, LANES), BF16)],
        compiler_params=_params("parallel", "parallel"), name="cast_in_proj",
    )(w)


def _rmsnorm_kernel(x_ref, g_ref, o_ref):
    x = x_ref[...]
    y = x * lax.rsqrt(jnp.mean(x * x, axis=-1, keepdims=True) + EPS)
    o_ref[...] = (y * g_ref[...]).astype(o_ref.dtype)


def _rmsnorm(x, g, out_dtype, row0=0, rows=None):
    M, D = x.shape
    rows = M if rows is None else rows
    tm = _pick_tile(math.gcd(rows, row0) if row0 else rows, 512, BF16_SUBLANES)
    rb0 = row0 // tm
    return pl.pallas_call(
        _rmsnorm_kernel,
        grid=(rows // tm,),
        in_specs=[pl.BlockSpec((tm, D), lambda i: (rb0 + i, 0)),
                  pl.BlockSpec((1, D), lambda i: (0, 0))],
        out_specs=pl.BlockSpec((tm, D), lambda i: (i, 0)),
        out_shape=jax.ShapeDtypeStruct((rows, D), out_dtype),
        compiler_params=_params("parallel"),
        name="rmsnorm",
    )(x, g.reshape(1, D))


def _mm_kernel(*refs, n_acts, pair_act, n_extra, combine):
    n_pairs = len(pair_act)
    w_refs = refs[n_acts:n_acts + n_pairs]
    extras = refs[n_acts + n_pairs:n_acts + n_pairs + n_extra]
    o_ref = refs[n_acts + n_pairs + n_extra]

    def dot(p):
        return jnp.dot(refs[pair_act[p]][...], w_refs[p][...], preferred_element_type=F32)

    o_ref[...] = combine(dot, extras).astype(o_ref.dtype)


def _fused_matmul(acts, pairs, extras, combine, layer, n_out, out_dtype, tm, tn, name):
    M = acts[0].shape[0]
    in_specs = [pl.BlockSpec((tm, a.shape[1]), lambda i, j: (i, 0)) for a in acts]
    args = list(acts)
    for ai, w, off in pairs:
        K = acts[ai].shape[1]
        assert off % tn == 0 and w.shape[1] == K
        in_specs.append(pl.BlockSpec((None, K, tn), lambda i, j, ob=off // tn: (layer, 0, j + ob)))
        args.append(w)
    for e in extras:
        in_specs.append(pl.BlockSpec((tm, tn), lambda i, j: (i, j)))
        args.append(e)
    return pl.pallas_call(
        functools.partial(_mm_kernel, n_acts=len(acts), pair_act=tuple(p[0] for p in pairs),
                          n_extra=len(extras), combine=combine),
        grid=(M // tm, n_out // tn),
        in_specs=in_specs,
        out_specs=pl.BlockSpec((tm, tn), lambda i, j: (i, j)),
        out_shape=jax.ShapeDtypeStruct((M, n_out), out_dtype),
        compiler_params=_params("parallel", "parallel"),
        name=name,
    )(*args)


def _mm_ksplit_kernel(a_ref, w_ref, x_ref, o_ref, acc_ref, *, nk):
    k = pl.program_id(2)
    d = jnp.dot(a_ref[...], w_ref[...], preferred_element_type=F32)

    @pl.when(k == 0)
    def _():
        acc_ref[...] = d

    @pl.when(k > 0)
    def _():
        acc_ref[...] += d

    @pl.when(k == nk - 1)
    def _():
        o_ref[...] = x_ref[...] + acc_ref[...]


def _residual_matmul_ksplit(a, w, layer, x, tm, tn, tk, name):
    M, K = a.shape
    N = w.shape[2]
    nk = K // tk
    return pl.pallas_call(
        functools.partial(_mm_ksplit_kernel, nk=nk),
        grid=(M // tm, N // tn, nk),
        in_specs=[pl.BlockSpec((tm, tk), lambda i, j, k: (i, k)),
                  pl.BlockSpec((None, tk, tn), lambda i, j, k: (layer, k, j)),
                  pl.BlockSpec((tm, tn), lambda i, j, k: (i, j))],
        out_specs=pl.BlockSpec((tm, tn), lambda i, j, k: (i, j)),
        out_shape=jax.ShapeDtypeStruct((M, N), F32),
        scratch_shapes=[pltpu.VMEM((tm, tn), F32)],
        compiler_params=_params("parallel", "parallel", "arbitrary"),
        name=name,
    )(a, w, x)


def _conv_gelu(a, a1, a2, wc_ref, bc_ref):
    acc = bc_ref[...] + a2 * wc_ref[0:1, :]
    acc = acc + a1 * wc_ref[1:2, :]
    acc = acc + a * wc_ref[2:3, :]
    return jax.nn.gelu(acc)


def _up_conv_kernel(h_ref, wa_ref, wg_ref, wc_ref, bc_ref, p1_ref, p2_ref, o_ref, halo_scr,
                    *, tm, Tp, Ts, Ms, n_i):
    i, j = pl.program_id(0), pl.program_id(1)

    @pl.when(i == 0)
    def _():
        halo_scr[j] = jnp.zeros(halo_scr.shape[1:], F32)

    h = h_ref[...]
    a = jnp.dot(h, wa_ref[...], preferred_element_type=F32)
    g = jnp.dot(h, wg_ref[...], preferred_element_type=F32)
    halo = halo_scr[j]
    prev1, prev2 = halo[SUBLANES - 1:SUBLANES, :], halo[SUBLANES - 2:SUBLANES - 1, :]
    halo_scr[j] = a[tm - SUBLANES:, :]
    r = lax.broadcasted_iota(jnp.int32, (tm, 1), 0)
    pos = (i * tm + r) % Tp
    a1 = jnp.where(r == 0, prev1, pltpu.roll(a, 1, 0))
    a2 = jnp.where(r == 0, prev2, jnp.where(r == 1, prev1, pltpu.roll(a, 2, 0)))
    a1 = jnp.where(pos == 0, 0.0, a1)
    a2 = jnp.where(pos < 2, 0.0, a2)
    o_ref[...] = (_conv_gelu(a, a1, a2, wc_ref, bc_ref) * g).astype(o_ref.dtype)

    @pl.when(i == n_i - 1)
    def _():
        lo = tm - Ms
        ps = lax.broadcasted_iota(jnp.int32, (Ms, 1), 0) % Ts
        a1s = jnp.where(ps == 0, p1_ref[...], a1[lo:, :])
        a2s = jnp.where(ps < 2, p2_ref[...], a2[lo:, :])
        o_ref[lo:, :] = (_conv_gelu(a[lo:, :], a1s, a2s, wc_ref, bc_ref) * g[lo:, :]).astype(o_ref.dtype)


def _up_conv(h, w_up, w_conv, b_conv, p1, p2, layer, Tp, Ts, tm, tf):
    M, K = h.shape
    F = w_conv.shape[2]
    Ms = p1.shape[0]
    n_i, n_f = M // tm, F // tf
    assert M % tm == 0 and F % tf == 0 and Ms <= tm and (tm - Ms) % BF16_SUBLANES == 0 and Ts >= 2
    return pl.pallas_call(
        functools.partial(_up_conv_kernel, tm=tm, Tp=Tp, Ts=Ts, Ms=Ms, n_i=n_i),
        grid=(n_i, n_f),
        in_specs=[pl.BlockSpec((tm, K), lambda i, j: (i, 0)),
                  pl.BlockSpec((None, K, tf), lambda i, j: (layer, 0, j)),
                  pl.BlockSpec((None, K, tf), lambda i, j: (layer, 0, n_f + j)),
                  pl.BlockSpec((None, w_conv.shape[1], tf), lambda i, j: (layer, 0, j)),
                  pl.BlockSpec((None, 1, tf), lambda i, j: (layer, 0, j)),
                  pl.BlockSpec((Ms, tf), lambda i, j: (0, j)),
                  pl.BlockSpec((Ms, tf), lambda i, j: (0, j))],
        out_specs=pl.BlockSpec((tm, tf), lambda i, j: (i, j)),
        out_shape=jax.ShapeDtypeStruct((M, F), BF16),
        scratch_shapes=[pltpu.VMEM((n_f, SUBLANES, tf), F32)],
        compiler_params=_params("arbitrary", "arbitrary"),
        name="up_conv",
    )(h, w_up, w_up, w_conv, b_conv, p1, p2)


def _gla_chunk(q, k, v, glr, w_a2, b_a, S, C):
    DK = q.shape[1]
    la = jnp.dot(glr.astype(BF16), w_a2, preferred_element_type=F32) + b_a
    log_a = (jnp.minimum(la, 0.0) - jnp.log1p(jnp.exp(-jnp.abs(la)))) / GLA_TAU
    row = lax.broadcasted_iota(jnp.int32, (C, DK), 0)
    b = log_a
    s = 1
    while s < C:
        b = b + jnp.where(row >= s, pltpu.roll(b, s, 0), 0.0)
        s *= 2
    b_last = b[C - 1:C, :]
    q_t = ((q * (DK ** -0.5)) * jnp.exp(b)).astype(BF16)
    k_t = (k * jnp.exp(-b)).astype(BF16)
    att = lax.dot_general(q_t, k_t, (((1,), (1,)), ((), ())), preferred_element_type=F32)
    ri = lax.broadcasted_iota(jnp.int32, (C, C), 0)
    ci = lax.broadcasted_iota(jnp.int32, (C, C), 1)
    att = jnp.where(ri >= ci, att, 0.0)
    vb = v.astype(BF16)
    o = (jnp.dot(att.astype(BF16), vb, preferred_element_type=F32)
         + jnp.dot(q_t, S.astype(BF16), preferred_element_type=F32))
    k_end = (k * jnp.exp(b_last - b)).astype(BF16)
    er = lax.broadcasted_iota(jnp.int32, (DK, DK), 0)
    ec = lax.broadcasted_iota(jnp.int32, (DK, DK), 1)
    decay_col = jnp.sum(jnp.where(er == ec, jnp.exp(b_last), 0.0), axis=1, keepdims=True)
    S_new = decay_col * S + lax.dot_general(k_end, vb, (((0,), (0,)), ((), ())),
                                            preferred_element_type=F32)
    return o, S_new


def _gla_kernel(q_ref, k_ref, v_ref, r_ref, lr_ref, wa_ref, ba_ref, go_ref, s0_ref, ybuf_ref,
                y_ref, sout_ref, s_scr, *, C, nb, rank, n_chunks):
    del ybuf_ref
    n = pl.program_id(2)

    @pl.when(n == 0)
    def _():
        s_scr[...] = s0_ref[...]

    outs = []
    for j in range(nb):
        rows = slice(j * C, (j + 1) * C)
        o, s_new = _gla_chunk(q_ref[rows, :], k_ref[rows, :], v_ref[rows, :], lr_ref[rows, :rank],
                              wa_ref[...], ba_ref[...], s_scr[j], C)
        s_scr[j] = s_new
        o = o * lax.rsqrt(jnp.mean(o * o, axis=-1, keepdims=True) + EPS) * go_ref[...]
        r = r_ref[rows, :]
        outs.append(o * (r * jax.nn.sigmoid(r)))
    y = outs[0] if nb == 1 else jnp.concatenate(outs, axis=0)
    y_ref[...] = y.astype(y_ref.dtype)

    @pl.when(n == n_chunks - 1)
    def _():
        sout_ref[...] = s_scr[...]


def _gla(z, zlr, w_a2, b_a, g_out, s0, ybuf, row0, T, off_q, off_k, off_v, off_r):
    B, H, DK, DV = s0.shape
    rank = w_a2.shape[0]
    C = math.gcd(T, GLA_CHUNK)
    n_chunks = T // C
    nb = 1 if C % BF16_SUBLANES == 0 else BF16_SUBLANES // C
    assert nb == 1 or n_chunks == 1
    assert B % nb == 0 and row0 % (nb * C) == 0
    R = nb * C
    rb0 = row0 // R

    def rows(b, h, n):
        return rb0 + b * n_chunks + n

    def zspec(width, off):
        assert off % width == 0
        return pl.BlockSpec((R, width), lambda b, h, n: (rows(b, h, n), off // width + h))

    return pl.pallas_call(
        functools.partial(_gla_kernel, C=C, nb=nb, rank=rank, n_chunks=n_chunks),
        grid=(B // nb, H, n_chunks),
        in_specs=[zspec(DK, off_q), zspec(DK, off_k), zspec(DV, off_v), zspec(DV, off_r),
                  pl.BlockSpec((R, zlr.shape[1]), lambda b, h, n: (rows(b, h, n), 0)),
                  pl.BlockSpec((rank, DK), lambda b, h, n: (0, h)),
                  pl.BlockSpec((1, DK), lambda b, h, n: (0, h)),
                  pl.BlockSpec((1, DV), lambda b, h, n: (0, h)),
                  pl.BlockSpec((nb, None, DK, DV), lambda b, h, n: (b, h, 0, 0)),
                  pl.BlockSpec(memory_space=pl.ANY)],
        out_specs=[pl.BlockSpec((R, DV), lambda b, h, n: (rows(b, h, n), h)),
                   pl.BlockSpec((nb, None, DK, DV), lambda b, h, n: (b, h, 0, 0))],
        out_shape=[jax.ShapeDtypeStruct(ybuf.shape, ybuf.dtype),
                   jax.ShapeDtypeStruct((B, H, DK, DV), F32)],
        scratch_shapes=[pltpu.VMEM((nb, DK, DV), F32)],
        input_output_aliases={9: 0},
        compiler_params=_params("parallel", "parallel", "arbitrary"),
        name="gla",
    )(z, z, z, z, zlr, w_a2, b_a, g_out, s0, ybuf)


def _swa_kernel(q_ref, k_ref, v_ref, kp_ref, vp_ref, slope_ref, sink_ref, ybuf_ref, y_ref,
                *, Tq, nb, KV, G, dh, first_block_has_no_past):
    del ybuf_ref
    n = pl.program_id(1)
    W = WINDOW
    rows, keys = G * Tq, W + Tq
    ri = lax.broadcasted_iota(jnp.int32, (rows, keys), 0)
    ci = lax.broadcasted_iota(jnp.int32, (rows, keys), 1)
    dist = (ri % Tq) + W - ci
    valid = (dist >= 0) & (dist < WINDOW)
    if first_block_has_no_past:
        valid = valid & ((ci >= W) | (n > 0))
    distf = dist.astype(F32)
    per_seq = []
    for j in range(nb):
        rs = slice(j * Tq, (j + 1) * Tq)
        q, k_own, v_own = q_ref[rs, :], k_ref[rs, :], v_ref[rs, :]
        k_past = kp_ref[j] if len(kp_ref.shape) == 3 else kp_ref[...]
        v_past = vp_ref[j] if len(vp_ref.shape) == 3 else vp_ref[...]
        heads = []
        for kv in range(KV):
            cs = slice(kv * dh, (kv + 1) * dh)
            kk = jnp.concatenate([k_past[:, cs], k_own[:, cs]], axis=0).astype(BF16)
            vv = jnp.concatenate([v_past[:, cs], v_own[:, cs]], axis=0).astype(BF16)
            qs = jnp.concatenate(
                [q[:, (kv * G + g) * dh:(kv * G + g + 1) * dh] for g in range(G)], axis=0).astype(BF16)
            s = lax.dot_general(qs, kk, (((1,), (1,)), ((), ())), preferred_element_type=F32)
            s = s * (dh ** -0.5) - slope_ref[kv] * distf
            s = jnp.where(valid, s, -jnp.inf)
            sink = sink_ref[kv]
            m = jnp.maximum(jnp.max(s, axis=1, keepdims=True), sink)
            e = jnp.exp(s - m)
            denom = jnp.sum(e, axis=1, keepdims=True) + jnp.exp(sink - m)
            o = jnp.dot((e / denom).astype(BF16), vv, preferred_element_type=F32)
            heads += [o[g * Tq:(g + 1) * Tq, :] for g in range(G)]
        per_seq.append(jnp.concatenate(heads, axis=1))
    y = per_seq[0] if nb == 1 else jnp.concatenate(per_seq, axis=0)
    y_ref[...] = y.astype(y_ref.dtype)


def _swa(z, k_past, v_past, slope_col, sink_col, ybuf, row0, B, T, Tq, KV, G, dh, off_q, off_k, off_v):
    nq = T // Tq
    nb = 1 if Tq % BF16_SUBLANES == 0 else BF16_SUBLANES // Tq
    assert nb == 1 or nq == 1
    R = nb * Tq
    assert B % nb == 0 and row0 % R == 0
    rb0 = row0 // R
    QW, KW = KV * G * dh, KV * dh
    assert off_q % QW == 0 and off_k % KW == 0 and off_v % KW == 0

    def own(width, off):
        return pl.BlockSpec((R, width), lambda b, n: (rb0 + b * nq + n, off // width))

    if k_past is None:
        assert Tq == WINDOW and nb == 1

        def past(off):
            return pl.BlockSpec((WINDOW, KW), lambda b, n: (jnp.maximum(rb0 + b * nq + n - 1, 0), off // KW))

        past_specs, past_args = [past(off_k), past(off_v)], [z, z]
    else:
        spec = pl.BlockSpec((nb, WINDOW, KW), lambda b, n: (b, 0, 0))
        past_specs, past_args = [spec, spec], [k_past, v_past]

    const = pl.BlockSpec((KV, G * Tq, 1), lambda b, n: (0, 0, 0))
    return pl.pallas_call(
        functools.partial(_swa_kernel, Tq=Tq, nb=nb, KV=KV, G=G, dh=dh,
                          first_block_has_no_past=k_past is None),
        grid=(B // nb, nq),
        in_specs=[own(QW, off_q), own(KW, off_k), own(KW, off_v)] + past_specs
                 + [const, const, pl.BlockSpec(memory_space=pl.ANY)],
        out_specs=pl.BlockSpec((R, QW), lambda b, n: (rb0 + b * nq + n, 0)),
        out_shape=jax.ShapeDtypeStruct(ybuf.shape, ybuf.dtype),
        input_output_aliases={7: 0},
        compiler_params=_params("parallel", "arbitrary"),
        name="swa",
    )(z, z, z, *past_args, slope_col, sink_col, ybuf)


def _pool_kernel(u_ref, halo_ref, w_ref, sc_ref, ybuf_ref, y_ref,
                 *, Tt, nb, P, G, Cg, first_block_has_no_past):
    del ybuf_ref
    n = pl.program_id(1)
    t_glob = n * Tt + lax.broadcasted_iota(jnp.int32, (Tt, 1), 0)
    per_seq = []
    for j in range(nb):
        u = u_ref[j * Tt:(j + 1) * Tt, :]
        halo = halo_ref[j] if len(halo_ref.shape) == 3 else halo_ref[...]
        if first_block_has_no_past:
            halo = jnp.where(n > 0, halo, 0.0)
        ext = jnp.concatenate([halo, u], axis=0)
        groups = []
        for g, w in enumerate(POOL_WINDOWS):
            cs = slice(g * Cg, (g + 1) * Cg)
            s = ext[:, cs]
            d = 1
            while d < w:
                s = s + pltpu.roll(s, d, 0)
                d *= 2
            count = jnp.minimum(w, P + 1 + t_glob).astype(F32)
            y = s[POOL_HALO:, :] / count - u[:, cs]
            groups.append(jnp.dot(y.astype(BF16), w_ref[g], preferred_element_type=F32))
        per_seq.append(jnp.concatenate(groups, axis=1) * sc_ref[...])
    y = per_seq[0] if nb == 1 else jnp.concatenate(per_seq, axis=0)
    y_ref[...] = y.astype(y_ref.dtype)


def _pool(z, prefix, w_pool, pool_scale, ybuf, row0, B, T, Tt, off_u):
    G, Cg, _ = w_pool.shape
    PD = G * Cg
    nt = T // Tt
    nb = 1 if Tt % BF16_SUBLANES == 0 else BF16_SUBLANES // Tt
    assert nb == 1 or nt == 1
    R = nb * Tt
    assert B % nb == 0 and row0 % R == 0 and off_u % PD == 0
    assert Tt % POOL_HALO == 0 or prefix is not None
    rb0 = row0 // R
    if prefix is None:
        hb = Tt // POOL_HALO
        halo_spec = pl.BlockSpec((POOL_HALO, PD),
                                 lambda b, n: (jnp.maximum((rb0 + b * nt + n) * hb - 1, 0), off_u // PD))
        halo_arg, P = z, 0
    else:
        halo_spec = pl.BlockSpec((nb, POOL_HALO, PD), lambda b, n: (b, 0, 0))
        halo_arg, P = prefix, POOL_HALO - 1
    return pl.pallas_call(
        functools.partial(_pool_kernel, Tt=Tt, nb=nb, P=P, G=G, Cg=Cg,
                          first_block_has_no_past=prefix is None),
        grid=(B // nb, nt),
        in_specs=[pl.BlockSpec((R, PD), lambda b, n: (rb0 + b * nt + n, off_u // PD)),
                  halo_spec,
                  pl.BlockSpec((G, Cg, Cg), lambda b, n: (0, 0, 0)),
                  pl.BlockSpec((1, PD), lambda b, n: (0, 0)),
                  pl.BlockSpec(memory_space=pl.ANY)],
        out_specs=pl.BlockSpec((R, PD), lambda b, n: (rb0 + b * nt + n, 0)),
        out_shape=jax.ShapeDtypeStruct(ybuf.shape, ybuf.dtype),
        input_output_aliases={4: 0},
        compiler_params=_params("parallel", "arbitrary"),
        name="pool",
    )(z, halo_arg, w_pool, pool_scale, ybuf)


def _plain(dot, extras):
    return dot(0)


def _gate_combine(dot, extras):
    acc = jax.nn.sigmoid(dot(0)) * dot(1)
    acc = acc + jax.nn.sigmoid(dot(2)) * dot(3)
    return acc + jax.nn.sigmoid(dot(4)) * dot(5)


def _residual(dot, extras):
    return extras[0][...] + dot(0)


def _ple(dot, extras):
    return extras[0][...] + dot(1) * jax.nn.sigmoid(dot(0))


def kernel(x_prompt, x_sample, state_gla, cache_swa_k, cache_swa_v, state_pool, state_ffn_conv, p_prompt, p_sample, g_mix, w_in, w_gla_a2, b_gla_a, g_gla_out, swa_sinks, w_pool, pool_scale, w_branch_gla, w_branch_swa, w_branch_pool, w_gate, w_out, g_ffn, w_up, w_conv, b_conv, w_down, g_ple, w_ple, w_ple_gate, g_final):
    Bp, Tp, D = x_prompt.shape
    Bs, Ts, _ = x_sample.shape
    depth = w_in.shape[0]
    _, _, H, DK, DV = state_gla.shape
    rank = w_gla_a2.shape[1]
    _, _, WB, KV, dh = cache_swa_k.shape
    NH = swa_sinks.shape[1]
    G = NH // KV
    _, PG, Cg, _ = w_pool.shape
    PD = PG * Cg
    P = state_pool.shape[2]
    F = w_down.shape[1]
    CW = w_conv.shape[1]
    assert WB == WINDOW and P == POOL_HALO - 1 and CW == 3 and PG == len(POOL_WINDOWS)
    assert Tp >= WINDOW and Tp >= P and Ts >= CW - 1
    GK, GV, SQ, SKV = H * DK, H * DV, NH * dh, KV * dh
    Mp, Ms = Bp * Tp, Bs * Ts
    M = Mp + Ms

    off_q, off_k, off_v, off_r = 0, GK, 2 * GK, 2 * GK + GV
    off_u = 2 * GK + 2 * GV
    off_sq = off_u + PD
    off_sk = off_sq + SQ
    off_sv = off_sk + SKV
    NZ = off_sv + SKV
    in_sizes = (GK, GK, GV, GV, rank, SQ, SKV, SKV, PD)
    in_off = [0]
    for s in in_sizes:
        in_off.append(in_off[-1] + s)
    segments = ((0, in_off[4]), (in_off[8], in_off[9]), (in_off[5], in_off[8]))

    tm = _pick_tile(M, 1056, BF16_SUBLANES)
    tm_half = _pick_tile(M, 528, BF16_SUBLANES)
    tf = _pick_tile(F, 256, LANES)
    tk_down = _pick_tile(F, 5504, LANES)
    tn_z = _pick_tile(NZ, 768, LANES)
    tn_d = _pick_tile(D, 512, LANES)
    tn_g = _pick_tile(D, 256, LANES)
    Tt_pool = _pick_tile(Tp, 256, POOL_HALO)
    M_tail = Bp * SUBLANES + Ms

    w_main_b, w_lr_b = _cast_in_proj(w_in, segments, (in_off[4], in_off[5]))
    w_gate_b = _cast_weights(w_gate)
    w_br_b = [_cast_weights(w_branch_gla), _cast_weights(w_branch_swa), _cast_weights(w_branch_pool)]
    w_out_b = _cast_weights(w_out)
    w_up_b = _cast_weights(w_up)
    w_down_b = _cast_weights(w_down)
    w_pg_b = _cast_weights(w_ple_gate)
    w_ple_b = w_ple.astype(BF16)
    w_a2_b = w_gla_a2.astype(BF16)
    w_pool_b = w_pool.astype(BF16)
    b_conv3 = b_conv.reshape(depth, 1, F)

    heads = jnp.arange(1, NH + 1, dtype=F32)
    slopes = jnp.exp2(-8.0 * heads / NH).reshape(KV, G, 1)

    def head_cols(v, tq):
        return jnp.broadcast_to(v[:, :, None, :], (KV, G, tq, 1)).reshape(KV, G * tq, 1)

    x = jnp.concatenate([x_prompt.reshape(Mp, D), x_sample.reshape(Ms, D)], axis=0)
    p_all = jnp.concatenate([p_prompt.reshape(depth, Mp, -1), p_sample.reshape(depth, Ms, -1)],
                            axis=1).astype(BF16)
    zero_state = jnp.zeros((Bp, H, DK, DV), F32)
    ya = jnp.zeros((M, GV), BF16)
    yb = jnp.zeros((M, SQ), BF16)
    yc = jnp.zeros((M, PD), BF16)
    tail_rows = [slice((b + 1) * Tp - SUBLANES, (b + 1) * Tp) for b in range(Bp)]
    outs_p, outs_s = [], []

    for i in range(depth):
        b_a = b_gla_a[i].reshape(1, GK)
        g_o = g_gla_out[i].reshape(1, GV)
        p_scale = pool_scale[i].reshape(1, PD)
        sinks = swa_sinks[i].reshape(KV, G, 1)

        h = _rmsnorm(x, g_mix[i], BF16)
        z = _fused_matmul([h], [(0, w_main_b, 0)], [], _plain, i, NZ, F32, tm, tn_z, "in_proj")
        zlr = _fused_matmul([h], [(0, w_lr_b, 0)], [], _plain, i, LANES, F32, tm, LANES, "in_proj_lr")

        gla_args = (w_a2_b[i], b_a, g_o)
        ya, gla_p = _gla(z, zlr, *gla_args, zero_state, ya, 0, Tp, off_q, off_k, off_v, off_r)
        ya, gla_s = _gla(z, zlr, *gla_args, state_gla[i], ya, Mp, Ts, off_q, off_k, off_v, off_r)

        swa_dims = (KV, G, dh, off_sq, off_sk, off_sv)
        yb = _swa(z, None, None, head_cols(slopes, WINDOW), head_cols(sinks, WINDOW), yb,
                  0, Bp, Tp, WINDOW, *swa_dims)
        ck = cache_swa_k[i].reshape(Bs, WB, SKV)
        cv = cache_swa_v[i].reshape(Bs, WB, SKV)
        yb = _swa(z, ck, cv, head_cols(slopes, Ts), head_cols(sinks, Ts), yb, Mp, Bs, Ts, Ts, *swa_dims)

        yc = _pool(z, None, w_pool_b[i], p_scale, yc, 0, Bp, Tp, Tt_pool, off_u)
        pool_prefix = jnp.pad(state_pool[i], ((0, 0), (POOL_HALO - P, 0), (0, 0)))
        yc = _pool(z, pool_prefix, w_pool_b[i], p_scale, yc, Mp, Bs, Ts, Ts, off_u)

        m = _fused_matmul([h, ya, yb, yc],
                          [(0, w_gate_b, 0), (1, w_br_b[0], 0), (0, w_gate_b, D), (2, w_br_b[1], 0),
                           (0, w_gate_b, 2 * D), (3, w_br_b[2], 0)],
                          [], _gate_combine, i, D, BF16, tm_half, tn_g, "gate_combine")
        x = _fused_matmul([m], [(0, w_out_b, 0)], [x], _residual, i, D, F32, tm, tn_d, "out_proj")

        h = _rmsnorm(x, g_ffn[i], BF16)
        pre = state_ffn_conv[i]
        hist = jnp.zeros((Bs, Ts, F), F32)
        p1 = hist.at[:, 0].set(pre[:, 1]).reshape(Ms, F)
        p2 = hist.at[:, 0].set(pre[:, 0]).at[:, 1].set(pre[:, 1]).reshape(Ms, F)
        act = _up_conv(h, w_up_b, w_conv, b_conv3, p1, p2, i, Tp, Ts, tm, tf)
        x = _residual_matmul_ksplit(act, w_down_b, i, x, tm, tn_d, tk_down, "down_proj")
        h_tail = jnp.concatenate([h[r] for r in tail_rows] + [h[Mp:]], axis=0)
        a_tail = _fused_matmul([h_tail], [(0, w_up_b, 0)], [], _plain, i, F, F32, M_tail, tf, "up_proj_tail")

        h = _rmsnorm(x, g_ple[i], BF16)
        x = _fused_matmul([h, p_all[i]], [(0, w_pg_b, 0), (1, w_ple_b, 0)], [x], _ple, i, D, F32,
                          tm, tn_d, "ple")

        wb = min(WINDOW, Tp)

        def prompt_tail(n_rows, off, width):
            return jnp.stack([z[(b + 1) * Tp - n_rows:(b + 1) * Tp, off:off + width] for b in range(Bp)])

        zs = z[Mp:].reshape(Bs, Ts, NZ)
        swa_k_p = prompt_tail(wb, off_sk, SKV).reshape(Bp, wb, KV, dh)
        swa_v_p = prompt_tail(wb, off_sv, SKV).reshape(Bp, wb, KV, dh)
        pool_p = prompt_tail(P, off_u, PD)
        conv_p = a_tail[:Bp * SUBLANES].reshape(Bp, SUBLANES, F)[:, SUBLANES - (CW - 1):]
        swa_k_s = jnp.concatenate([ck, zs[:, :, off_sk:off_sk + SKV]], axis=1)[:, Ts:].reshape(Bs, WB, KV, dh)
        swa_v_s = jnp.concatenate([cv, zs[:, :, off_sv:off_sv + SKV]], axis=1)[:, Ts:].reshape(Bs, WB, KV, dh)
        pool_s = jnp.concatenate([state_pool[i], zs[:, :, off_u:off_u + PD]], axis=1)[:, Ts:]
        conv_s = a_tail[Bp * SUBLANES:].reshape(Bs, Ts, F)[:, Ts - (CW - 1):]
        outs_p.append((gla_p, swa_k_p, swa_v_p, pool_p, conv_p))
        outs_s.append((gla_s, swa_k_s, swa_v_s, pool_s, conv_s))

    y_prompt = _rmsnorm(x, g_final, F32, 0, Mp).reshape(Bp, Tp, D)
    y_sample = _rmsnorm(x, g_final, F32, Mp, Ms).reshape(Bs, Ts, D)
    stack = lambda outs, c: jnp.stack([o[c] for o in outs])
    return (y_prompt, y_sample,
            stack(outs_p, 0), stack(outs_p, 1), stack(outs_p, 2), stack(outs_p, 3), stack(outs_p, 4),
            stack(outs_s, 0), stack(outs_s, 1), stack(outs_s, 2), stack(outs_s, 3), stack(outs_s, 4))
```

```python
import functools
import math

import jax
import jax.numpy as jnp
from jax import lax
from jax.experimental import pallas as pl
from jax.experimental.pallas import tpu as pltpu

EPS = 1e-6
GLA_TAU = 16.0
GLA_CHUNK = 64
WINDOW = 128
POOL_WINDOWS = (2, 4, 8, 16)

VMEM_LIMIT_BYTES = 56 * 1024 * 1024
LANES = 128
SUBLANES = 8
BF16_SUBLANES = 16
POOL_HALO = 16
CAST_BLOCK_BYTES = 6 * 1024 * 1024

F32 = jnp.float32
BF16 = jnp.bfloat16


def _pick_tile(n, target, align):
    best = None
    for t in range(align, min(n, target) + 1, align):
        if n % t == 0:
            best = t
    return n if best is None else best


def _params(*sem):
    return pltpu.CompilerParams(dimension_semantics=sem, vmem_limit_bytes=VMEM_LIMIT_BYTES)


def _cast_kernel(w_ref, o_ref):
    o_ref[...] = w_ref[...].astype(o_ref.dtype)


def _cast_weights(w):
    L, K, N = w.shape
    tk = _pick_tile(K, max(BF16_SUBLANES, CAST_BLOCK_BYTES // (4 * N)), BF16_SUBLANES)
    spec = pl.BlockSpec((None, tk, N), lambda l, k: (l, k, 0))
    return pl.pallas_call(
        _cast_kernel, grid=(L, K // tk), in_specs=[spec], out_specs=spec,
        out_shape=jax.ShapeDtypeStruct((L, K, N), BF16),
        compiler_params=_params("parallel", "parallel"), name="cast_weights",
    )(w)


def _cast_in_proj_kernel(wt_ref, main_ref, lr_ref, *, segments, lr):
    wt = wt_ref[...]
    main = jnp.concatenate([wt[a:b, :] for a, b in segments], axis=0)
    main_ref[...] = main.T.astype(main_ref.dtype)
    pad = jnp.zeros((lr_ref.shape[1] - (lr[1] - lr[0]), wt.shape[1]), F32)
    lr_ref[...] = jnp.concatenate([wt[lr[0]:lr[1], :], pad], axis=0).T.astype(lr_ref.dtype)


def _cast_in_proj(wt, segments, lr):
    L, N, K = wt.shape
    n_main = sum(b - a for a, b in segments)
    tk = _pick_tile(K, LANES, LANES)
    return pl.pallas_call(
        functools.partial(_cast_in_proj_kernel, segments=segments, lr=lr),
        grid=(L, K // tk),
        in_specs=[pl.BlockSpec((None, N, tk), lambda l, k: (l, 0, k))],
        out_specs=[pl.BlockSpec((None, tk, n_main), lambda l, k: (l, k, 0)),
                   pl.BlockSpec((None, tk, LANES), lambda l, k: (l, k, 0))],
        out_shape=[jax.ShapeDtypeStruct((L, K, n_main), BF16),
                   jax.ShapeDtypeStruct((L, K, LANES), BF16)],
        compiler_params=_params("parallel", "parallel"), name="cast_in_proj",
    )(wt)


def _rmsnorm_kernel(x_ref, g_ref, o_ref):
    x = x_ref[...]
    y = x * lax.rsqrt(jnp.mean(x * x, axis=-1, keepdims=True) + EPS)
    o_ref[...] = (y * g_ref[...]).astype(o_ref.dtype)


def _rmsnorm(x, g, out_dtype, row0=0, rows=None):
    M, D = x.shape
    rows = M if rows is None else rows
    tm = _pick_tile(math.gcd(rows, row0) if row0 else rows, 512, BF16_SUBLANES)
    rb0 = row0 // tm
    return pl.pallas_call(
        _rmsnorm_kernel,
        grid=(rows // tm,),
        in_specs=[pl.BlockSpec((tm, D), lambda i: (rb0 + i, 0)),
                  pl.BlockSpec((1, D), lambda i: (0, 0))],
        out_specs=pl.BlockSpec((tm, D), lambda i: (i, 0)),
        out_shape=jax.ShapeDtypeStruct((rows, D), out_dtype),
        compiler_params=_params("parallel"),
        name="rmsnorm",
    )(x, g.reshape(1, D))


def _mm_kernel(*refs, n_acts, pair_act, n_extra, combine):
    n_pairs = len(pair_act)
    w_refs = refs[n_acts:n_acts + n_pairs]
    extras = refs[n_acts + n_pairs:n_acts + n_pairs + n_extra]
    o_ref = refs[n_acts + n_pairs + n_extra]

    def dot(p):
        return jnp.dot(refs[pair_act[p]][...], w_refs[p][...], preferred_element_type=F32)

    o_ref[...] = combine(dot, extras).astype(o_ref.dtype)


def _fused_matmul(acts, pairs, extras, combine, layer, n_out, out_dtype, tm, tn, name):
    M = acts[0].shape[0]
    in_specs = [pl.BlockSpec((tm, a.shape[1]), lambda i, j: (i, 0)) for a in acts]
    args = list(acts)
    for ai, w, off in pairs:
        K = acts[ai].shape[1]
        assert off % tn == 0 and w.shape[1] == K
        in_specs.append(pl.BlockSpec((None, K, tn), lambda i, j, ob=off // tn: (layer, 0, j + ob)))
        args.append(w)
    for e in extras:
        in_specs.append(pl.BlockSpec((tm, tn), lambda i, j: (i, j)))
        args.append(e)
    return pl.pallas_call(
        functools.partial(_mm_kernel, n_acts=len(acts), pair_act=tuple(p[0] for p in pairs),
                          n_extra=len(extras), combine=combine),
        grid=(M // tm, n_out // tn),
        in_specs=in_specs,
        out_specs=pl.BlockSpec((tm, tn), lambda i, j: (i, j)),
        out_shape=jax.ShapeDtypeStruct((M, n_out), out_dtype),
        compiler_params=_params("parallel", "parallel"),
        name=name,
    )(*args)


def _mm_ksplit_kernel(a_ref, w_ref, x_ref, o_ref, acc_ref, *, nk):
    k = pl.program_id(2)
    d = jnp.dot(a_ref[...], w_ref[...], preferred_element_type=F32)

    @pl.when(k == 0)
    def _():
        acc_ref[...] = d

    @pl.when(k > 0)
    def _():
        acc_ref[...] += d

    @pl.when(k == nk - 1)
    def _():
        o_ref[...] = x_ref[...] + acc_ref[...]


def _residual_matmul_ksplit(a, w, layer, x, tm, tn, tk, name):
    M, K = a.shape
    N = w.shape[2]
    nk = K // tk
    return pl.pallas_call(
        functools.partial(_mm_ksplit_kernel, nk=nk),
        grid=(M // tm, N // tn, nk),
        in_specs=[pl.BlockSpec((tm, tk), lambda i, j, k: (i, k)),
                  pl.BlockSpec((None, tk, tn), lambda i, j, k: (layer, k, j)),
                  pl.BlockSpec((tm, tn), lambda i, j, k: (i, j))],
        out_specs=pl.BlockSpec((tm, tn), lambda i, j, k: (i, j)),
        out_shape=jax.ShapeDtypeStruct((M, N), F32),
        scratch_shapes=[pltpu.VMEM((tm, tn), F32)],
        compiler_params=_params("parallel", "parallel", "arbitrary"),
        name=name,
    )(a, w, x)


def _conv_gelu(a, a1, a2, wc_ref, bc_ref):
    acc = bc_ref[...] + a2 * wc_ref[0:1, :]
    acc = acc + a1 * wc_ref[1:2, :]
    acc = acc + a * wc_ref[2:3, :]
    return jax.nn.gelu(acc)


def _up_conv_kernel(h_ref, wa_ref, wg_ref, wc_ref, bc_ref, r0_ref, r1_ref, o_ref,
                    a_scr, g_scr, halo_scr, *, tm, Mp, Tp, Ts, Ms, piece):
    i, j = pl.program_id(0), pl.program_id(1)

    @pl.when(i == 0)
    def _():
        halo_scr[j] = jnp.zeros(halo_scr.shape[1:], F32)

    h = h_ref[...]
    a_scr[0:SUBLANES, :] = halo_scr[j]
    a_scr[SUBLANES:, :] = jnp.dot(h, wa_ref[...], preferred_element_type=F32)
    g_scr[...] = jnp.dot(h, wg_ref[...], preferred_element_type=F32)
    halo_scr[j] = a_scr[tm:, :]

    row0 = i * tm
    sub = lax.broadcasted_iota(jnp.int32, (SUBLANES, 1), 0)
    for lo in range(0, tm, piece):
        a = a_scr[lo + SUBLANES:lo + SUBLANES + piece, :]
        a1 = a_scr[lo + SUBLANES - 1:lo + SUBLANES - 1 + piece, :]
        a2 = a_scr[lo + SUBLANES - 2:lo + SUBLANES - 2 + piece, :]
        if lo < tm - Ms:
            starts = (row0 + lo) % Tp == 0
            top1 = jnp.where(starts & (sub == 0), 0.0, a1[:SUBLANES, :])
            top2 = jnp.where(starts & (sub < 2), 0.0, a2[:SUBLANES, :])
            a1 = jnp.concatenate([top1, a1[SUBLANES:, :]], axis=0)
            a2 = jnp.concatenate([top2, a2[SUBLANES:, :]], axis=0)
        else:
            row = row0 + lo + lax.broadcasted_iota(jnp.int32, (piece, 1), 0)
            is_sample = row >= Mp
            pos = jnp.where(is_sample, (row - Mp) % Ts, row % Tp)
            ls = lo - (tm - Ms)
            hist2 = jnp.where(is_sample, r0_ref[ls:ls + piece, :], 0.0)
            hist1 = jnp.where(is_sample, r1_ref[ls:ls + piece, :], 0.0)
            a1 = jnp.where(pos == 0, hist1, a1)
            a2 = jnp.where(pos == 0, hist2, jnp.where(pos == 1, hist1, a2))
        act = _conv_gelu(a, a1, a2, wc_ref, bc_ref) * g_scr[lo:lo + piece, :]
        o_ref[lo:lo + piece, :] = act.astype(o_ref.dtype)


def _up_conv(h, w_up, w_conv, b_conv, r0, r1, layer, Mp, Tp, Ts, tm, tf):
    M, K = h.shape
    F = w_conv.shape[2]
    Ms = r0.shape[0]
    n_i, n_f = M // tm, F // tf
    piece = BF16_SUBLANES
    assert M % tm == 0 and F % tf == 0 and M - Mp == Ms and Ms <= tm
    assert tm % piece == 0 and (tm - Ms) % piece == 0 and Tp % piece == 0
    return pl.pallas_call(
        functools.partial(_up_conv_kernel, tm=tm, Mp=Mp, Tp=Tp, Ts=Ts, Ms=Ms, piece=piece),
        grid=(n_i, n_f),
        in_specs=[pl.BlockSpec((tm, K), lambda i, j: (i, 0)),
                  pl.BlockSpec((None, K, tf), lambda i, j: (layer, 0, j)),
                  pl.BlockSpec((None, K, tf), lambda i, j: (layer, 0, n_f + j)),
                  pl.BlockSpec((None, w_conv.shape[1], tf), lambda i, j: (layer, 0, j)),
                  pl.BlockSpec((None, 1, tf), lambda i, j: (layer, 0, j)),
                  pl.BlockSpec((Ms, tf), lambda i, j: (0, j)),
                  pl.BlockSpec((Ms, tf), lambda i, j: (0, j))],
        out_specs=pl.BlockSpec((tm, tf), lambda i, j: (i, j)),
        out_shape=jax.ShapeDtypeStruct((M, F), BF16),
        scratch_shapes=[pltpu.VMEM((SUBLANES + tm, tf), F32), pltpu.VMEM((tm, tf), F32),
                        pltpu.VMEM((n_f, SUBLANES, tf), F32)],
        compiler_params=_params("arbitrary", "arbitrary"),
        name="up_conv",
    )(h, w_up, w_up, w_conv, b_conv, r0, r1)


def _gla_chunk(q, k, v, glr, w_a2, b_a, S, C):
    DK = q.shape[1]
    la = jnp.dot(glr.astype(BF16), w_a2, preferred_element_type=F32) + b_a
    log_a = (jnp.minimum(la, 0.0) - jnp.log1p(jnp.exp(-jnp.abs(la)))) / GLA_TAU
    row = lax.broadcasted_iota(jnp.int32, (C, DK), 0)
    b = log_a
    s = 1
    while s < C:
        b = b + jnp.where(row >= s, pltpu.roll(b, s, 0), 0.0)
        s *= 2
    b_last = b[C - 1:C, :]
    q_t = ((q * (DK ** -0.5)) * jnp.exp(b)).astype(BF16)
    k_t = (k * jnp.exp(-b)).astype(BF16)
    att = lax.dot_general(q_t, k_t, (((1,), (1,)), ((), ())), preferred_element_type=F32)
    ri = lax.broadcasted_iota(jnp.int32, (C, C), 0)
    ci = lax.broadcasted_iota(jnp.int32, (C, C), 1)
    att = jnp.where(ri >= ci, att, 0.0)
    vb = v.astype(BF16)
    o = (jnp.dot(att.astype(BF16), vb, preferred_element_type=F32)
         + jnp.dot(q_t, S.astype(BF16), preferred_element_type=F32))
    k_end = (k * jnp.exp(b_last - b)).astype(BF16)
    er = lax.broadcasted_iota(jnp.int32, (DK, DK), 0)
    ec = lax.broadcasted_iota(jnp.int32, (DK, DK), 1)
    decay_col = jnp.sum(jnp.where(er == ec, jnp.exp(b_last), 0.0), axis=1, keepdims=True)
    S_new = decay_col * S + lax.dot_general(k_end, vb, (((0,), (0,)), ((), ())),
                                            preferred_element_type=F32)
    return o, S_new


def _gla_kernel(*refs, C, nb, H, DK, DV, rank, n_chunks, has_s0):
    q_ref, k_ref, v_ref, r_ref, lr_ref, wa_ref, ba_ref, go_ref = refs[:8]
    s0_ref = refs[8] if has_s0 else None
    y_ref, sout_ref, s_scr = refs[-3:]
    n = pl.program_id(1)

    @pl.when(n == 0)
    def _():
        s_scr[...] = s0_ref[...] if has_s0 else jnp.zeros(s_scr.shape, F32)

    per_seq = []
    for j in range(nb):
        rows = slice(j * C, (j + 1) * C)
        glr = lr_ref[rows, :rank]
        heads = []
        for h in range(H):
            kc, vc = slice(h * DK, (h + 1) * DK), slice(h * DV, (h + 1) * DV)
            o, s_new = _gla_chunk(q_ref[rows, kc], k_ref[rows, kc], v_ref[rows, vc], glr,
                                  wa_ref[:, kc], ba_ref[:, kc], s_scr[j, h], C)
            s_scr[j, h] = s_new
            o = o * lax.rsqrt(jnp.mean(o * o, axis=-1, keepdims=True) + EPS) * go_ref[:, vc]
            r = r_ref[rows, vc]
            heads.append(o * (r * jax.nn.sigmoid(r)))
        per_seq.append(jnp.concatenate(heads, axis=1))
    y = per_seq[0] if nb == 1 else jnp.concatenate(per_seq, axis=0)
    y_ref[...] = y.astype(y_ref.dtype)

    @pl.when(n == n_chunks - 1)
    def _():
        sout_ref[...] = s_scr[...]


def _gla(z, zlr, w_a2, b_a, g_out, s0, ybuf, sbuf, layer, row0, B, T, H, DK, DV, off_q, off_k, off_v, off_r):
    rank = w_a2.shape[0]
    GK, GV = H * DK, H * DV
    C = math.gcd(T, GLA_CHUNK)
    n_chunks = T // C
    nb = 1 if C % BF16_SUBLANES == 0 else BF16_SUBLANES // C
    assert nb == 1 or n_chunks == 1
    assert B % nb == 0 and row0 % (nb * C) == 0
    assert off_q % GK == 0 and off_k % GK == 0 and off_v % GV == 0 and off_r % GV == 0
    R = nb * C
    rb0 = row0 // R

    def zspec(width, off):
        return pl.BlockSpec((R, width), lambda b, n: (rb0 + b * n_chunks + n, off // width))

    state_spec = pl.BlockSpec((None, nb, H, DK, DV), lambda b, n: (layer, b, 0, 0, 0))
    full = lambda a: pl.BlockSpec(a.shape, lambda b, n: (0,) * a.ndim)
    any_spec = pl.BlockSpec(memory_space=pl.ANY)
    in_specs = [zspec(GK, off_q), zspec(GK, off_k), zspec(GV, off_v), zspec(GV, off_r),
                pl.BlockSpec((R, zlr.shape[1]), lambda b, n: (rb0 + b * n_chunks + n, 0)),
                full(w_a2), full(b_a), full(g_out)]
    args = [z, z, z, z, zlr, w_a2, b_a, g_out]
    if s0 is not None:
        in_specs.append(state_spec)
        args.append(s0)
    n_in = len(args)
    return pl.pallas_call(
        functools.partial(_gla_kernel, C=C, nb=nb, H=H, DK=DK, DV=DV, rank=rank, n_chunks=n_chunks,
                          has_s0=s0 is not None),
        grid=(B // nb, n_chunks),
        in_specs=in_specs + [any_spec, any_spec],
        out_specs=[pl.BlockSpec((R, GV), lambda b, n: (rb0 + b * n_chunks + n, 0)), state_spec],
        out_shape=[jax.ShapeDtypeStruct(ybuf.shape, ybuf.dtype),
                   jax.ShapeDtypeStruct(sbuf.shape, sbuf.dtype)],
        scratch_shapes=[pltpu.VMEM((nb, H, DK, DV), F32)],
        input_output_aliases={n_in: 0, n_in + 1: 1},
        compiler_params=_params("parallel", "arbitrary"),
        name="gla",
    )(*args, ybuf, sbuf)


def _swa_kernel(q_ref, k_ref, v_ref, kp_ref, vp_ref, slope_ref, sink_ref, ybuf_ref, y_ref,
                *, Tq, nb, KV, G, dh, first_block_has_no_past):
    del ybuf_ref
    n = pl.program_id(1)
    W = WINDOW
    rows, keys = G * Tq, W + Tq
    ri = lax.broadcasted_iota(jnp.int32, (rows, keys), 0)
    ci = lax.broadcasted_iota(jnp.int32, (rows, keys), 1)
    dist = (ri % Tq) + W - ci
    valid = (dist >= 0) & (dist < WINDOW)
    if first_block_has_no_past:
        valid = valid & ((ci >= W) | (n > 0))
    distf = dist.astype(F32)
    per_seq = []
    for j in range(nb):
        rs = slice(j * Tq, (j + 1) * Tq)
        q, k_own, v_own = q_ref[rs, :], k_ref[rs, :], v_ref[rs, :]
        k_past = kp_ref[j] if len(kp_ref.shape) == 3 else kp_ref[...]
        v_past = vp_ref[j] if len(vp_ref.shape) == 3 else vp_ref[...]
        heads = []
        for kv in range(KV):
            cs = slice(kv * dh, (kv + 1) * dh)
            kk = jnp.concatenate([k_past[:, cs], k_own[:, cs]], axis=0).astype(BF16)
            vv = jnp.concatenate([v_past[:, cs], v_own[:, cs]], axis=0).astype(BF16)
            qs = jnp.concatenate(
                [q[:, (kv * G + g) * dh:(kv * G + g + 1) * dh] for g in range(G)], axis=0).astype(BF16)
            s = lax.dot_general(qs, kk, (((1,), (1,)), ((), ())), preferred_element_type=F32)
            s = s * (dh ** -0.5) - slope_ref[kv] * distf
            s = jnp.where(valid, s, -jnp.inf)
            sink = sink_ref[kv]
            m = jnp.maximum(jnp.max(s, axis=1, keepdims=True), sink)
            e = jnp.exp(s - m)
            denom = jnp.sum(e, axis=1, keepdims=True) + jnp.exp(sink - m)
            o = jnp.dot((e / denom).astype(BF16), vv, preferred_element_type=F32)
            heads += [o[g * Tq:(g + 1) * Tq, :] for g in range(G)]
        per_seq.append(jnp.concatenate(heads, axis=1))
    y = per_seq[0] if nb == 1 else jnp.concatenate(per_seq, axis=0)
    y_ref[...] = y.astype(y_ref.dtype)


def _swa(z, k_past, v_past, slope_col, sink_col, ybuf, row0, B, T, Tq, KV, G, dh, off_q, off_k, off_v):
    nq = T // Tq
    nb = 1 if Tq % BF16_SUBLANES == 0 else BF16_SUBLANES // Tq
    assert nb == 1 or nq == 1
    R = nb * Tq
    assert B % nb == 0 and row0 % R == 0
    rb0 = row0 // R
    QW, KW = KV * G * dh, KV * dh
    assert off_q % QW == 0 and off_k % KW == 0 and off_v % KW == 0

    def own(width, off):
        return pl.BlockSpec((R, width), lambda b, n: (rb0 + b * nq + n, off // width))

    if k_past is None:
        assert Tq == WINDOW and nb == 1

        def past(off):
            return pl.BlockSpec((WINDOW, KW), lambda b, n: (jnp.maximum(rb0 + b * nq + n - 1, 0), off // KW))

        past_specs, past_args = [past(off_k), past(off_v)], [z, z]
    else:
        spec = pl.BlockSpec((nb, WINDOW, KW), lambda b, n: (b, 0, 0))
        past_specs, past_args = [spec, spec], [k_past, v_past]

    const = pl.BlockSpec((KV, G * Tq, 1), lambda b, n: (0, 0, 0))
    return pl.pallas_call(
        functools.partial(_swa_kernel, Tq=Tq, nb=nb, KV=KV, G=G, dh=dh,
                          first_block_has_no_past=k_past is None),
        grid=(B // nb, nq),
        in_specs=[own(QW, off_q), own(KW, off_k), own(KW, off_v)] + past_specs
                 + [const, const, pl.BlockSpec(memory_space=pl.ANY)],
        out_specs=pl.BlockSpec((R, QW), lambda b, n: (rb0 + b * nq + n, 0)),
        out_shape=jax.ShapeDtypeStruct(ybuf.shape, ybuf.dtype),
        input_output_aliases={7: 0},
        compiler_params=_params("parallel", "arbitrary"),
        name="swa",
    )(z, z, z, *past_args, slope_col, sink_col, ybuf)


def _pool_kernel(u_ref, halo_ref, w_ref, sc_ref, ybuf_ref, y_ref,
                 *, Tt, nb, P, G, Cg, first_block_has_no_past):
    del ybuf_ref
    n = pl.program_id(1)
    t_glob = n * Tt + lax.broadcasted_iota(jnp.int32, (Tt, 1), 0)
    per_seq = []
    for j in range(nb):
        u = u_ref[j * Tt:(j + 1) * Tt, :]
        halo = halo_ref[j] if len(halo_ref.shape) == 3 else halo_ref[...]
        if first_block_has_no_past:
            halo = jnp.where(n > 0, halo, 0.0)
        ext = jnp.concatenate([halo, u], axis=0)
        groups = []
        for g, w in enumerate(POOL_WINDOWS):
            cs = slice(g * Cg, (g + 1) * Cg)
            s = ext[:, cs]
            d = 1
            while d < w:
                s = s + pltpu.roll(s, d, 0)
                d *= 2
            count = jnp.minimum(w, P + 1 + t_glob).astype(F32)
            y = s[POOL_HALO:, :] / count - u[:, cs]
            groups.append(jnp.dot(y.astype(BF16), w_ref[g], preferred_element_type=F32))
        per_seq.append(jnp.concatenate(groups, axis=1) * sc_ref[...])
    y = per_seq[0] if nb == 1 else jnp.concatenate(per_seq, axis=0)
    y_ref[...] = y.astype(y_ref.dtype)


def _pool(z, prefix, w_pool, pool_scale, ybuf, row0, B, T, Tt, off_u):
    G, Cg, _ = w_pool.shape
    PD = G * Cg
    nt = T // Tt
    nb = 1 if Tt % BF16_SUBLANES == 0 else BF16_SUBLANES // Tt
    assert nb == 1 or nt == 1
    R = nb * Tt
    assert B % nb == 0 and row0 % R == 0 and off_u % PD == 0
    assert Tt % POOL_HALO == 0 or prefix is not None
    rb0 = row0 // R
    if prefix is None:
        hb = Tt // POOL_HALO
        halo_spec = pl.BlockSpec((POOL_HALO, PD),
                                 lambda b, n: (jnp.maximum((rb0 + b * nt + n) * hb - 1, 0), off_u // PD))
        halo_arg, P = z, 0
    else:
        halo_spec = pl.BlockSpec((nb, POOL_HALO, PD), lambda b, n: (b, 0, 0))
        halo_arg, P = prefix, POOL_HALO - 1
    return pl.pallas_call(
        functools.partial(_pool_kernel, Tt=Tt, nb=nb, P=P, G=G, Cg=Cg,
                          first_block_has_no_past=prefix is None),
        grid=(B // nb, nt),
        in_specs=[pl.BlockSpec((R, PD), lambda b, n: (rb0 + b * nt + n, off_u // PD)),
                  halo_spec,
                  pl.BlockSpec((G, Cg, Cg), lambda b, n: (0, 0, 0)),
                  pl.BlockSpec((1, PD), lambda b, n: (0, 0)),
                  pl.BlockSpec(memory_space=pl.ANY)],
        out_specs=pl.BlockSpec((R, PD), lambda b, n: (rb0 + b * nt + n, 0)),
        out_shape=jax.ShapeDtypeStruct(ybuf.shape, ybuf.dtype),
        input_output_aliases={4: 0},
        compiler_params=_params("parallel", "arbitrary"),
        name="pool",
    )(z, halo_arg, w_pool, pool_scale, ybuf)


def _plain(dot, extras):
    return dot(0)


def _gate_combine(dot, extras):
    acc = jax.nn.sigmoid(dot(0)) * dot(1)
    acc = acc + jax.nn.sigmoid(dot(2)) * dot(3)
    return acc + jax.nn.sigmoid(dot(4)) * dot(5)


def _residual(dot, extras):
    return extras[0][...] + dot(0)


def _ple(dot, extras):
    return extras[0][...] + dot(1) * jax.nn.sigmoid(dot(0))


def kernel(x_prompt, x_sample, state_gla, cache_swa_k, cache_swa_v, state_pool, state_ffn_conv, p_prompt, p_sample, g_mix, w_in, w_gla_a2, b_gla_a, g_gla_out, swa_sinks, w_pool, pool_scale, w_branch_gla, w_branch_swa, w_branch_pool, w_gate, w_out, g_ffn, w_up, w_conv, b_conv, w_down, g_ple, w_ple, w_ple_gate, g_final):
    Bp, Tp, D = x_prompt.shape
    Bs, Ts, _ = x_sample.shape
    depth = w_in.shape[0]
    _, _, H, DK, DV = state_gla.shape
    rank = w_gla_a2.shape[1]
    _, _, WB, KV, dh = cache_swa_k.shape
    NH = swa_sinks.shape[1]
    G = NH // KV
    _, PG, Cg, _ = w_pool.shape
    PD = PG * Cg
    P = state_pool.shape[2]
    F = w_down.shape[1]
    CW = w_conv.shape[1]
    assert WB == WINDOW and P == POOL_HALO - 1 and CW == 3 and PG == len(POOL_WINDOWS)
    assert Tp >= WINDOW and Tp >= P and Ts >= CW - 1
    GK, GV, SQ, SKV = H * DK, H * DV, NH * dh, KV * dh
    Mp, Ms = Bp * Tp, Bs * Ts
    M = Mp + Ms

    off_q, off_k, off_v, off_r = 0, GK, 2 * GK, 2 * GK + GV
    off_u = 2 * GK + 2 * GV
    off_sq = off_u + PD
    off_sk = off_sq + SQ
    off_sv = off_sk + SKV
    NZ = off_sv + SKV
    in_sizes = (GK, GK, GV, GV, rank, SQ, SKV, SKV, PD)
    in_off = [0]
    for s in in_sizes:
        in_off.append(in_off[-1] + s)
    segments = ((0, in_off[4]), (in_off[8], in_off[9]), (in_off[5], in_off[8]))

    tm = _pick_tile(M, 1056, BF16_SUBLANES)
    tm_half = _pick_tile(M, 528, BF16_SUBLANES)
    tf = _pick_tile(F, 256, LANES)
    tk_down = _pick_tile(F, 5504, LANES)
    tn_z = _pick_tile(NZ, 768, LANES)
    tn_d = _pick_tile(D, 512, LANES)
    tn_g = _pick_tile(D, 256, LANES)
    Tt_pool = _pick_tile(Tp, 256, POOL_HALO)
    M_tail = Bp * SUBLANES + Ms

    w_main_b, w_lr_b = _cast_in_proj(jnp.swapaxes(w_in, 1, 2), segments, (in_off[4], in_off[5]))
    w_gate_b = _cast_weights(w_gate)
    w_br_b = [_cast_weights(w_branch_gla), _cast_weights(w_branch_swa), _cast_weights(w_branch_pool)]
    w_out_b = _cast_weights(w_out)
    w_up_b = _cast_weights(w_up)
    w_down_b = _cast_weights(w_down)
    w_pg_b = _cast_weights(w_ple_gate)
    w_ple_b = w_ple.astype(BF16)
    w_a2_b = w_gla_a2.astype(BF16)
    w_pool_b = w_pool.astype(BF16)
    b_conv3 = b_conv.reshape(depth, 1, F)

    heads = jnp.arange(1, NH + 1, dtype=F32)
    slopes = jnp.exp2(-8.0 * heads / NH).reshape(KV, G, 1)

    def head_cols(v, tq):
        return jnp.broadcast_to(v[:, :, None, :], (KV, G, tq, 1)).reshape(KV, G * tq, 1)

    x = jnp.concatenate([x_prompt.reshape(Mp, D), x_sample.reshape(Ms, D)], axis=0)
    p_all = jnp.concatenate([p_prompt.reshape(depth, Mp, -1), p_sample.reshape(depth, Ms, -1)],
                            axis=1).astype(BF16)
    gla_p = jnp.zeros((depth, Bp, H, DK, DV), F32)
    gla_s = jnp.zeros((depth, Bs, H, DK, DV), F32)
    ya = jnp.zeros((M, GV), BF16)
    yb = jnp.zeros((M, SQ), BF16)
    yc = jnp.zeros((M, PD), BF16)
    tail_rows = [slice((b + 1) * Tp - SUBLANES, (b + 1) * Tp) for b in range(Bp)]
    outs_p, outs_s = [], []

    for i in range(depth):
        b_a = b_gla_a[i].reshape(1, GK)
        g_o = g_gla_out[i].reshape(1, GV)
        p_scale = pool_scale[i].reshape(1, PD)
        sinks = swa_sinks[i].reshape(KV, G, 1)

        h = _rmsnorm(x, g_mix[i], BF16)
        z = _fused_matmul([h], [(0, w_main_b, 0)], [], _plain, i, NZ, F32, tm, tn_z, "in_proj")
        zlr = _fused_matmul([h], [(0, w_lr_b, 0)], [], _plain, i, LANES, F32, tm, LANES, "in_proj_lr")

        gla_args = (w_a2_b[i], b_a, g_o)
        gla_dims = (H, DK, DV, off_q, off_k, off_v, off_r)
        ya, gla_p = _gla(z, zlr, *gla_args, None, ya, gla_p, i, 0, Bp, Tp, *gla_dims)
        ya, gla_s = _gla(z, zlr, *gla_args, state_gla, ya, gla_s, i, Mp, Bs, Ts, *gla_dims)

        swa_dims = (KV, G, dh, off_sq, off_sk, off_sv)
        yb = _swa(z, None, None, head_cols(slopes, WINDOW), head_cols(sinks, WINDOW), yb,
                  0, Bp, Tp, WINDOW, *swa_dims)
        ck = cache_swa_k[i].reshape(Bs, WB, SKV)
        cv = cache_swa_v[i].reshape(Bs, WB, SKV)
        yb = _swa(z, ck, cv, head_cols(slopes, Ts), head_cols(sinks, Ts), yb, Mp, Bs, Ts, Ts, *swa_dims)

        yc = _pool(z, None, w_pool_b[i], p_scale, yc, 0, Bp, Tp, Tt_pool, off_u)
        pool_prefix = jnp.pad(state_pool[i], ((0, 0), (POOL_HALO - P, 0), (0, 0)))
        yc = _pool(z, pool_prefix, w_pool_b[i], p_scale, yc, Mp, Bs, Ts, Ts, off_u)

        m = _fused_matmul([h, ya, yb, yc],
                          [(0, w_gate_b, 0), (1, w_br_b[0], 0), (0, w_gate_b, D), (2, w_br_b[1], 0),
                           (0, w_gate_b, 2 * D), (3, w_br_b[2], 0)],
                          [], _gate_combine, i, D, BF16, tm_half, tn_g, "gate_combine")
        x = _fused_matmul([m], [(0, w_out_b, 0)], [x], _residual, i, D, F32, tm, tn_d, "out_proj")

        h = _rmsnorm(x, g_ffn[i], BF16)
        pre = state_ffn_conv[i]
        r0 = jnp.broadcast_to(pre[:, 0:1], (Bs, Ts, F)).reshape(Ms, F)
        r1 = jnp.broadcast_to(pre[:, 1:2], (Bs, Ts, F)).reshape(Ms, F)
        act = _up_conv(h, w_up_b, w_conv, b_conv3, r0, r1, i, Mp, Tp, Ts, tm, tf)
        x = _residual_matmul_ksplit(act, w_down_b, i, x, tm, tn_d, tk_down, "down_proj")
        h_tail = jnp.concatenate([h[r] for r in tail_rows] + [h[Mp:]], axis=0)
        a_tail = _fused_matmul([h_tail], [(0, w_up_b, 0)], [], _plain, i, F, F32, M_tail, tf, "up_proj_tail")

        h = _rmsnorm(x, g_ple[i], BF16)
        x = _fused_matmul([h, p_all[i]], [(0, w_pg_b, 0), (1, w_ple_b, 0)], [x], _ple, i, D, F32,
                          tm, tn_d, "ple")

        wb = min(WINDOW, Tp)

        def prompt_tail(n_rows, off, width):
            return jnp.stack([z[(b + 1) * Tp - n_rows:(b + 1) * Tp, off:off + width] for b in range(Bp)])

        zs = z[Mp:].reshape(Bs, Ts, NZ)
        swa_k_p = prompt_tail(wb, off_sk, SKV).reshape(Bp, wb, KV, dh)
        swa_v_p = prompt_tail(wb, off_sv, SKV).reshape(Bp, wb, KV, dh)
        pool_p = prompt_tail(P, off_u, PD)
        conv_p = a_tail[:Bp * SUBLANES].reshape(Bp, SUBLANES, F)[:, SUBLANES - (CW - 1):]
        swa_k_s = jnp.concatenate([ck, zs[:, :, off_sk:off_sk + SKV]], axis=1)[:, Ts:].reshape(Bs, WB, KV, dh)
        swa_v_s = jnp.concatenate([cv, zs[:, :, off_sv:off_sv + SKV]], axis=1)[:, Ts:].reshape(Bs, WB, KV, dh)
        pool_s = jnp.concatenate([state_pool[i], zs[:, :, off_u:off_u + PD]], axis=1)[:, Ts:]
        conv_s = a_tail[Bp * SUBLANES:].reshape(Bs, Ts, F)[:, Ts - (CW - 1):]
        outs_p.append((swa_k_p, swa_v_p, pool_p, conv_p))
        outs_s.append((swa_k_s, swa_v_s, pool_s, conv_s))

    y_prompt = _rmsnorm(x, g_final, F32, 0, Mp).reshape(Bp, Tp, D)
    y_sample = _rmsnorm(x, g_final, F32, Mp, Ms).reshape(Bs, Ts, D)
    stack = lambda outs, c: jnp.stack([o[c] for o in outs])
    return (y_prompt, y_sample,
            gla_p, stack(outs_p, 0), stack(outs_p, 1), stack(outs_p, 2), stack(outs_p, 3),
            gla_s, stack(outs_s, 0), stack(outs_s, 1), stack(outs_s, 2), stack(outs_s, 3))
```

```python
import functools
import math

import jax
import jax.numpy as jnp
from jax import lax
from jax.experimental import pallas as pl
from jax.experimental.pallas import tpu as pltpu

EPS = 1e-6
GLA_TAU = 16.0
GLA_CHUNK = 64
WINDOW = 128
POOL_WINDOWS = (2, 4, 8, 16)

VMEM_LIMIT_BYTES = 56 * 1024 * 1024
LANES = 128
SUBLANES = 8
BF16_SUBLANES = 16
POOL_HALO = 16
CAST_BLOCK_BYTES = 6 * 1024 * 1024

F32 = jnp.float32
BF16 = jnp.bfloat16


def _pick_tile(n, target, align):
    best = None
    for t in range(align, min(n, target) + 1, align):
        if n % t == 0:
            best = t
    return n if best is None else best


def _params(*sem):
    return pltpu.CompilerParams(dimension_semantics=sem, vmem_limit_bytes=VMEM_LIMIT_BYTES)


def _cast_kernel(w_ref, o_ref):
    o_ref[...] = w_ref[...].astype(o_ref.dtype)


def _cast_weights(w):
    L, K, N = w.shape
    tk = _pick_tile(K, max(BF16_SUBLANES, CAST_BLOCK_BYTES // (4 * N)), BF16_SUBLANES)
    spec = pl.BlockSpec((None, tk, N), lambda l, k: (l, k, 0))
    return pl.pallas_call(
        _cast_kernel, grid=(L, K // tk), in_specs=[spec], out_specs=spec,
        out_shape=jax.ShapeDtypeStruct((L, K, N), BF16),
        compiler_params=_params("parallel", "parallel"), name="cast_weights",
    )(w)


def _cast_in_proj_kernel(wt_ref, main_ref, lr_ref, *, segments, lr):
    wt = wt_ref[...]
    main = jnp.concatenate([wt[a:b, :] for a, b in segments], axis=0)
    main_ref[...] = main.T.astype(main_ref.dtype)
    pad = jnp.zeros((lr_ref.shape[1] - (lr[1] - lr[0]), wt.shape[1]), F32)
    lr_ref[...] = jnp.concatenate([wt[lr[0]:lr[1], :], pad], axis=0).T.astype(lr_ref.dtype)


def _cast_in_proj(wt, segments, lr):
    L, N, K = wt.shape
    n_main = sum(b - a for a, b in segments)
    tk = _pick_tile(K, LANES, LANES)
    return pl.pallas_call(
        functools.partial(_cast_in_proj_kernel, segments=segments, lr=lr),
        grid=(L, K // tk),
        in_specs=[pl.BlockSpec((None, N, tk), lambda l, k: (l, 0, k))],
        out_specs=[pl.BlockSpec((None, tk, n_main), lambda l, k: (l, k, 0)),
                   pl.BlockSpec((None, tk, LANES), lambda l, k: (l, k, 0))],
        out_shape=[jax.ShapeDtypeStruct((L, K, n_main), BF16),
                   jax.ShapeDtypeStruct((L, K, LANES), BF16)],
        compiler_params=_params("parallel", "parallel"), name="cast_in_proj",
    )(wt)


def _rmsnorm_kernel(x_ref, g_ref, o_ref):
    x = x_ref[...]
    y = x * lax.rsqrt(jnp.mean(x * x, axis=-1, keepdims=True) + EPS)
    o_ref[...] = (y * g_ref[...]).astype(o_ref.dtype)


def _rmsnorm(x, g, out_dtype, row0=0, rows=None):
    M, D = x.shape
    rows = M if rows is None else rows
    tm = _pick_tile(math.gcd(rows, row0) if row0 else rows, 512, BF16_SUBLANES)
    rb0 = row0 // tm
    return pl.pallas_call(
        _rmsnorm_kernel,
        grid=(rows // tm,),
        in_specs=[pl.BlockSpec((tm, D), lambda i: (rb0 + i, 0)),
                  pl.BlockSpec((1, D), lambda i: (0, 0))],
        out_specs=pl.BlockSpec((tm, D), lambda i: (i, 0)),
        out_shape=jax.ShapeDtypeStruct((rows, D), out_dtype),
        compiler_params=_params("parallel"),
        name="rmsnorm",
    )(x, g.reshape(1, D))


def _mm_kernel(*refs, n_acts, pair_act, n_extra, combine):
    n_pairs = len(pair_act)
    w_refs = refs[n_acts:n_acts + n_pairs]
    extras = refs[n_acts + n_pairs:n_acts + n_pairs + n_extra]
    o_ref = refs[n_acts + n_pairs + n_extra]

    def dot(p):
        return jnp.dot(refs[pair_act[p]][...], w_refs[p][...], preferred_element_type=F32)

    o_ref[...] = combine(dot, extras).astype(o_ref.dtype)


def _fused_matmul(acts, pairs, extras, combine, layer, n_out, out_dtype, tm, tn, name):
    M = acts[0].shape[0]
    in_specs = [pl.BlockSpec((tm, a.shape[1]), lambda i, j: (i, 0)) for a in acts]
    args = list(acts)
    for ai, w, off in pairs:
        K = acts[ai].shape[1]
        assert off % tn == 0 and w.shape[1] == K
        in_specs.append(pl.BlockSpec((None, K, tn), lambda i, j, ob=off // tn: (layer, 0, j + ob)))
        args.append(w)
    for e in extras:
        in_specs.append(pl.BlockSpec((tm, tn), lambda i, j: (i, j)))
        args.append(e)
    return pl.pallas_call(
        functools.partial(_mm_kernel, n_acts=len(acts), pair_act=tuple(p[0] for p in pairs),
                          n_extra=len(extras), combine=combine),
        grid=(M // tm, n_out // tn),
        in_specs=in_specs,
        out_specs=pl.BlockSpec((tm, tn), lambda i, j: (i, j)),
        out_shape=jax.ShapeDtypeStruct((M, n_out), out_dtype),
        compiler_params=_params("parallel", "parallel"),
        name=name,
    )(*args)


def _mm_ksplit_kernel(a_ref, w_ref, x_ref, o_ref, acc_ref, *, nk):
    k = pl.program_id(2)
    d = jnp.dot(a_ref[...], w_ref[...], preferred_element_type=F32)

    @pl.when(k == 0)
    def _():
        acc_ref[...] = d

    @pl.when(k > 0)
    def _():
        acc_ref[...] += d

    @pl.when(k == nk - 1)
    def _():
        o_ref[...] = x_ref[...] + acc_ref[...]


def _residual_matmul_ksplit(a, w, layer, x, tm, tn, tk, name):
    M, K = a.shape
    N = w.shape[2]
    nk = K // tk
    return pl.pallas_call(
        functools.partial(_mm_ksplit_kernel, nk=nk),
        grid=(M // tm, N // tn, nk),
        in_specs=[pl.BlockSpec((tm, tk), lambda i, j, k: (i, k)),
                  pl.BlockSpec((None, tk, tn), lambda i, j, k: (layer, k, j)),
                  pl.BlockSpec((tm, tn), lambda i, j, k: (i, j))],
        out_specs=pl.BlockSpec((tm, tn), lambda i, j, k: (i, j)),
        out_shape=jax.ShapeDtypeStruct((M, N), F32),
        scratch_shapes=[pltpu.VMEM((tm, tn), F32)],
        compiler_params=_params("parallel", "parallel", "arbitrary"),
        name=name,
    )(a, w, x)


GELU_K0 = -2.0 * math.sqrt(2.0 / math.pi) * math.log2(math.e)
GELU_K1 = 0.044715 * GELU_K0


def _conv_gelu(a, a1, a2, wc_ref, bc_ref):
    x = bc_ref[...] + a2 * wc_ref[0:1, :]
    x = x + a1 * wc_ref[1:2, :]
    x = x + a * wc_ref[2:3, :]
    return x / (1.0 + jnp.exp2(x * (GELU_K0 + GELU_K1 * (x * x))))


def _up_conv_kernel(h_ref, wa_ref, wg_ref, wc_ref, bc_ref, r0_ref, r1_ref, o_ref,
                    a_scr, g_scr, halo_scr, *, tm, Mp, Tp, Ts, Ms, piece):
    i, j = pl.program_id(0), pl.program_id(1)

    @pl.when(i == 0)
    def _():
        halo_scr[j] = jnp.zeros(halo_scr.shape[1:], F32)

    h = h_ref[...]
    a_scr[0:SUBLANES, :] = halo_scr[j]
    a_scr[SUBLANES:, :] = jnp.dot(h, wa_ref[...], preferred_element_type=F32)
    g_scr[...] = jnp.dot(h, wg_ref[...], preferred_element_type=F32)
    halo_scr[j] = a_scr[tm:, :]

    row0 = i * tm
    sub = lax.broadcasted_iota(jnp.int32, (SUBLANES, 1), 0)
    for lo in range(0, tm, piece):
        ext = a_scr[lo:lo + SUBLANES + piece, :]
        a = ext[SUBLANES:, :]
        a1 = pltpu.roll(ext, 1, 0)[SUBLANES:, :]
        a2 = pltpu.roll(ext, 2, 0)[SUBLANES:, :]
        if lo < tm - Ms:
            starts = (row0 + lo) % Tp == 0
            top1 = jnp.where(starts & (sub == 0), 0.0, a1[:SUBLANES, :])
            top2 = jnp.where(starts & (sub < 2), 0.0, a2[:SUBLANES, :])
            a1 = jnp.concatenate([top1, a1[SUBLANES:, :]], axis=0)
            a2 = jnp.concatenate([top2, a2[SUBLANES:, :]], axis=0)
        else:
            row = row0 + lo + lax.broadcasted_iota(jnp.int32, (piece, 1), 0)
            is_sample = row >= Mp
            pos = jnp.where(is_sample, (row - Mp) % Ts, row % Tp)
            ls = lo - (tm - Ms)
            hist2 = jnp.where(is_sample, r0_ref[ls:ls + piece, :], 0.0)
            hist1 = jnp.where(is_sample, r1_ref[ls:ls + piece, :], 0.0)
            a1 = jnp.where(pos == 0, hist1, a1)
            a2 = jnp.where(pos == 0, hist2, jnp.where(pos == 1, hist1, a2))
        act = _conv_gelu(a, a1, a2, wc_ref, bc_ref) * g_scr[lo:lo + piece, :]
        o_ref[lo:lo + piece, :] = act.astype(o_ref.dtype)


def _up_conv(h, w_up, w_conv, b_conv, r0, r1, layer, Mp, Tp, Ts, tm, tf):
    M, K = h.shape
    F = w_conv.shape[2]
    Ms = r0.shape[0]
    n_i, n_f = M // tm, F // tf
    piece = next(p for p in (2 * BF16_SUBLANES, BF16_SUBLANES) if tm % p == 0 and (tm - Ms) % p == 0)
    assert M % tm == 0 and F % tf == 0 and M - Mp == Ms and Ms <= tm
    assert tm % piece == 0 and (tm - Ms) % piece == 0 and Tp % piece == 0
    return pl.pallas_call(
        functools.partial(_up_conv_kernel, tm=tm, Mp=Mp, Tp=Tp, Ts=Ts, Ms=Ms, piece=piece),
        grid=(n_i, n_f),
        in_specs=[pl.BlockSpec((tm, K), lambda i, j: (i, 0)),
                  pl.BlockSpec((None, K, tf), lambda i, j: (layer, 0, j)),
                  pl.BlockSpec((None, K, tf), lambda i, j: (layer, 0, n_f + j)),
                  pl.BlockSpec((None, w_conv.shape[1], tf), lambda i, j: (layer, 0, j)),
                  pl.BlockSpec((None, 1, tf), lambda i, j: (layer, 0, j)),
                  pl.BlockSpec((Ms, tf), lambda i, j: (0, j)),
                  pl.BlockSpec((Ms, tf), lambda i, j: (0, j))],
        out_specs=pl.BlockSpec((tm, tf), lambda i, j: (i, j)),
        out_shape=jax.ShapeDtypeStruct((M, F), BF16),
        scratch_shapes=[pltpu.VMEM((SUBLANES + tm, tf), F32), pltpu.VMEM((tm, tf), F32),
                        pltpu.VMEM((n_f, SUBLANES, tf), F32)],
        compiler_params=_params("arbitrary", "arbitrary"),
        name="up_conv",
    )(h, w_up, w_up, w_conv, b_conv, r0, r1)


def _gla_chunk(q, k, v, glr, w_a2, b_a, S, C):
    DK = q.shape[1]
    la = jnp.dot(glr.astype(BF16), w_a2, preferred_element_type=F32) + b_a
    log_a = (jnp.minimum(la, 0.0) - jnp.log1p(jnp.exp(-jnp.abs(la)))) / GLA_TAU
    row = lax.broadcasted_iota(jnp.int32, (C, DK), 0)
    b = log_a
    s = 1
    while s < C:
        b = b + jnp.where(row >= s, pltpu.roll(b, s, 0), 0.0)
        s *= 2
    b_last = b[C - 1:C, :]
    q_t = ((q * (DK ** -0.5)) * jnp.exp(b)).astype(BF16)
    k_t = (k * jnp.exp(-b)).astype(BF16)
    att = lax.dot_general(q_t, k_t, (((1,), (1,)), ((), ())), preferred_element_type=F32)
    ri = lax.broadcasted_iota(jnp.int32, (C, C), 0)
    ci = lax.broadcasted_iota(jnp.int32, (C, C), 1)
    att = jnp.where(ri >= ci, att, 0.0)
    vb = v.astype(BF16)
    o = (jnp.dot(att.astype(BF16), vb, preferred_element_type=F32)
         + jnp.dot(q_t, S.astype(BF16), preferred_element_type=F32))
    k_end = (k * jnp.exp(b_last - b)).astype(BF16)
    er = lax.broadcasted_iota(jnp.int32, (DK, DK), 0)
    ec = lax.broadcasted_iota(jnp.int32, (DK, DK), 1)
    decay_col = jnp.sum(jnp.where(er == ec, jnp.exp(b_last), 0.0), axis=1, keepdims=True)
    S_new = decay_col * S + lax.dot_general(k_end, vb, (((0,), (0,)), ((), ())),
                                            preferred_element_type=F32)
    return o, S_new


def _gla_kernel(*refs, C, nb, H, DK, DV, rank, n_chunks, has_s0):
    q_ref, k_ref, v_ref, r_ref, lr_ref, wa_ref, ba_ref, go_ref = refs[:8]
    s0_ref = refs[8] if has_s0 else None
    y_ref, sout_ref, s_scr = refs[-3:]
    n = pl.program_id(1)

    @pl.when(n == 0)
    def _():
        s_scr[...] = s0_ref[...] if has_s0 else jnp.zeros(s_scr.shape, F32)

    per_seq = []
    for j in range(nb):
        rows = slice(j * C, (j + 1) * C)
        glr = lr_ref[rows, :rank]
        heads = []
        for h in range(H):
            kc, vc = slice(h * DK, (h + 1) * DK), slice(h * DV, (h + 1) * DV)
            o, s_new = _gla_chunk(q_ref[rows, kc], k_ref[rows, kc], v_ref[rows, vc], glr,
                                  wa_ref[:, kc], ba_ref[:, kc], s_scr[j, h], C)
            s_scr[j, h] = s_new
            o = o * lax.rsqrt(jnp.mean(o * o, axis=-1, keepdims=True) + EPS) * go_ref[:, vc]
            r = r_ref[rows, vc]
            heads.append(o * (r * jax.nn.sigmoid(r)))
        per_seq.append(jnp.concatenate(heads, axis=1))
    y = per_seq[0] if nb == 1 else jnp.concatenate(per_seq, axis=0)
    y_ref[...] = y.astype(y_ref.dtype)

    @pl.when(n == n_chunks - 1)
    def _():
        sout_ref[...] = s_scr[...]


def _gla(z, zlr, w_a2, b_a, g_out, s0, ybuf, sbuf, layer, row0, B, T, H, DK, DV, off_q, off_k, off_v, off_r):
    rank = w_a2.shape[0]
    GK, GV = H * DK, H * DV
    C = math.gcd(T, GLA_CHUNK)
    n_chunks = T // C
    nb = 1 if C % BF16_SUBLANES == 0 else BF16_SUBLANES // C
    assert nb == 1 or n_chunks == 1
    assert B % nb == 0 and row0 % (nb * C) == 0
    assert off_q % GK == 0 and off_k % GK == 0 and off_v % GV == 0 and off_r % GV == 0
    R = nb * C
    rb0 = row0 // R

    def zspec(width, off):
        return pl.BlockSpec((R, width), lambda b, n: (rb0 + b * n_chunks + n, off // width))

    state_spec = pl.BlockSpec((None, nb, H, DK, DV), lambda b, n: (layer, b, 0, 0, 0))
    full = lambda a: pl.BlockSpec(a.shape, lambda b, n: (0,) * a.ndim)
    any_spec = pl.BlockSpec(memory_space=pl.ANY)
    in_specs = [zspec(GK, off_q), zspec(GK, off_k), zspec(GV, off_v), zspec(GV, off_r),
                pl.BlockSpec((R, zlr.shape[1]), lambda b, n: (rb0 + b * n_chunks + n, 0)),
                full(w_a2), full(b_a), full(g_out)]
    args = [z, z, z, z, zlr, w_a2, b_a, g_out]
    if s0 is not None:
        in_specs.append(state_spec)
        args.append(s0)
    n_in = len(args)
    return pl.pallas_call(
        functools.partial(_gla_kernel, C=C, nb=nb, H=H, DK=DK, DV=DV, rank=rank, n_chunks=n_chunks,
                          has_s0=s0 is not None),
        grid=(B // nb, n_chunks),
        in_specs=in_specs + [any_spec, any_spec],
        out_specs=[pl.BlockSpec((R, GV), lambda b, n: (rb0 + b * n_chunks + n, 0)), state_spec],
        out_shape=[jax.ShapeDtypeStruct(ybuf.shape, ybuf.dtype),
                   jax.ShapeDtypeStruct(sbuf.shape, sbuf.dtype)],
        scratch_shapes=[pltpu.VMEM((nb, H, DK, DV), F32)],
        input_output_aliases={n_in: 0, n_in + 1: 1},
        compiler_params=_params("parallel", "arbitrary"),
        name="gla",
    )(*args, ybuf, sbuf)


def _swa_block_kernel(q_ref, k_ref, v_ref, kp_ref, vp_ref, bias_ref, sink_ref, ybuf_ref, y_ref,
                      *, KV, G, dh):
    del ybuf_ref
    scale = dh ** -0.5
    fold_scale = math.frexp(scale)[0] == 0.5
    heads = []
    for kv in range(KV):
        cs = slice(kv * dh, (kv + 1) * dh)
        kk = jnp.concatenate([kp_ref[:, cs], k_ref[:, cs]], axis=0).astype(BF16)
        vv = jnp.concatenate([vp_ref[:, cs], v_ref[:, cs]], axis=0).astype(BF16)
        for g in range(G):
            h = kv * G + g
            q = q_ref[:, h * dh:(h + 1) * dh]
            q = (q * scale if fold_scale else q).astype(BF16)
            s = lax.dot_general(q, kk, (((1,), (1,)), ((), ())), preferred_element_type=F32)
            s = (s if fold_scale else s * scale) + bias_ref[h]
            sink = sink_ref[h]
            m = jnp.maximum(jnp.max(s, axis=1, keepdims=True), sink)
            e = jnp.exp(s - m)
            denom = jnp.sum(e, axis=1, keepdims=True) + jnp.exp(sink - m)
            heads.append(jnp.dot((e / denom).astype(BF16), vv, preferred_element_type=F32))
    y_ref[...] = jnp.concatenate(heads, axis=1).astype(y_ref.dtype)


def _swa_blocks(z, bias, sinks, ybuf, B, T, KV, G, dh, off_q, off_k, off_v):
    nq = T // WINDOW
    QW, KW = KV * G * dh, KV * dh
    assert T % WINDOW == 0 and off_q % QW == 0 and off_k % KW == 0 and off_v % KW == 0

    def own(width, off):
        return pl.BlockSpec((WINDOW, width), lambda b, n: (b * nq + n, off // width))

    def past(off):
        return pl.BlockSpec((WINDOW, KW), lambda b, n: (jnp.maximum(b * nq + n - 1, 0), off // KW))

    return pl.pallas_call(
        functools.partial(_swa_block_kernel, KV=KV, G=G, dh=dh),
        grid=(B, nq),
        in_specs=[own(QW, off_q), own(KW, off_k), own(KW, off_v), past(off_k), past(off_v),
                  pl.BlockSpec((None,) + bias.shape[1:], lambda b, n: (jnp.minimum(n, 1), 0, 0, 0)),
                  pl.BlockSpec(memory_space=pltpu.SMEM),
                  pl.BlockSpec(memory_space=pl.ANY)],
        out_specs=pl.BlockSpec((WINDOW, QW), lambda b, n: (b * nq + n, 0)),
        out_shape=jax.ShapeDtypeStruct(ybuf.shape, ybuf.dtype),
        input_output_aliases={7: 0},
        compiler_params=_params("parallel", "arbitrary"),
        name="swa_blocks",
    )(z, z, z, z, z, bias, sinks, ybuf)


def _swa_kernel(q_ref, k_ref, v_ref, kp_ref, vp_ref, slope_ref, sink_ref, ybuf_ref, y_ref,
                *, Tq, nb, KV, G, dh, first_block_has_no_past):
    del ybuf_ref
    n = pl.program_id(1)
    W = WINDOW
    rows, keys = G * Tq, W + Tq
    ri = lax.broadcasted_iota(jnp.int32, (rows, keys), 0)
    ci = lax.broadcasted_iota(jnp.int32, (rows, keys), 1)
    dist = (ri % Tq) + W - ci
    valid = (dist >= 0) & (dist < WINDOW)
    if first_block_has_no_past:
        valid = valid & ((ci >= W) | (n > 0))
    distf = dist.astype(F32)
    per_seq = []
    for j in range(nb):
        rs = slice(j * Tq, (j + 1) * Tq)
        q, k_own, v_own = q_ref[rs, :], k_ref[rs, :], v_ref[rs, :]
        k_past = kp_ref[j] if len(kp_ref.shape) == 3 else kp_ref[...]
        v_past = vp_ref[j] if len(vp_ref.shape) == 3 else vp_ref[...]
        heads = []
        for kv in range(KV):
            cs = slice(kv * dh, (kv + 1) * dh)
            kk = jnp.concatenate([k_past[:, cs], k_own[:, cs]], axis=0).astype(BF16)
            vv = jnp.concatenate([v_past[:, cs], v_own[:, cs]], axis=0).astype(BF16)
            qs = jnp.concatenate(
                [q[:, (kv * G + g) * dh:(kv * G + g + 1) * dh] for g in range(G)], axis=0).astype(BF16)
            s = lax.dot_general(qs, kk, (((1,), (1,)), ((), ())), preferred_element_type=F32)
            s = s * (dh ** -0.5) - slope_ref[kv] * distf
            s = jnp.where(valid, s, -jnp.inf)
            sink = sink_ref[kv]
            m = jnp.maximum(jnp.max(s, axis=1, keepdims=True), sink)
            e = jnp.exp(s - m)
            denom = jnp.sum(e, axis=1, keepdims=True) + jnp.exp(sink - m)
            o = jnp.dot((e / denom).astype(BF16), vv, preferred_element_type=F32)
            heads += [o[g * Tq:(g + 1) * Tq, :] for g in range(G)]
        per_seq.append(jnp.concatenate(heads, axis=1))
    y = per_seq[0] if nb == 1 else jnp.concatenate(per_seq, axis=0)
    y_ref[...] = y.astype(y_ref.dtype)


def _swa(z, k_past, v_past, slope_col, sink_col, ybuf, row0, B, T, Tq, KV, G, dh, off_q, off_k, off_v):
    nq = T // Tq
    nb = 1 if Tq % BF16_SUBLANES == 0 else BF16_SUBLANES // Tq
    assert nb == 1 or nq == 1
    R = nb * Tq
    assert B % nb == 0 and row0 % R == 0
    rb0 = row0 // R
    QW, KW = KV * G * dh, KV * dh
    assert off_q % QW == 0 and off_k % KW == 0 and off_v % KW == 0

    def own(width, off):
        return pl.BlockSpec((R, width), lambda b, n: (rb0 + b * nq + n, off // width))

    if k_past is None:
        assert Tq == WINDOW and nb == 1

        def past(off):
            return pl.BlockSpec((WINDOW, KW), lambda b, n: (jnp.maximum(rb0 + b * nq + n - 1, 0), off // KW))

        past_specs, past_args = [past(off_k), past(off_v)], [z, z]
    else:
        spec = pl.BlockSpec((nb, WINDOW, KW), lambda b, n: (b, 0, 0))
        past_specs, past_args = [spec, spec], [k_past, v_past]

    const = pl.BlockSpec((KV, G * Tq, 1), lambda b, n: (0, 0, 0))
    return pl.pallas_call(
        functools.partial(_swa_kernel, Tq=Tq, nb=nb, KV=KV, G=G, dh=dh,
                          first_block_has_no_past=k_past is None),
        grid=(B // nb, nq),
        in_specs=[own(QW, off_q), own(KW, off_k), own(KW, off_v)] + past_specs
                 + [const, const, pl.BlockSpec(memory_space=pl.ANY)],
        out_specs=pl.BlockSpec((R, QW), lambda b, n: (rb0 + b * nq + n, 0)),
        out_shape=jax.ShapeDtypeStruct(ybuf.shape, ybuf.dtype),
        input_output_aliases={7: 0},
        compiler_params=_params("parallel", "arbitrary"),
        name="swa",
    )(z, z, z, *past_args, slope_col, sink_col, ybuf)


def _pool_kernel(u_ref, halo_ref, w_ref, sc_ref, ybuf_ref, y_ref,
                 *, Tt, nb, P, G, Cg, first_block_has_no_past):
    del ybuf_ref
    n = pl.program_id(1)
    t_glob = n * Tt + lax.broadcasted_iota(jnp.int32, (Tt, 1), 0)
    per_seq = []
    for j in range(nb):
        u = u_ref[j * Tt:(j + 1) * Tt, :]
        halo = halo_ref[j] if len(halo_ref.shape) == 3 else halo_ref[...]
        if first_block_has_no_past:
            halo = jnp.where(n > 0, halo, 0.0)
        ext = jnp.concatenate([halo, u], axis=0)
        groups = []
        for g, w in enumerate(POOL_WINDOWS):
            cs = slice(g * Cg, (g + 1) * Cg)
            s = ext[:, cs]
            d = 1
            while d < w:
                s = s + pltpu.roll(s, d, 0)
                d *= 2
            count = jnp.minimum(w, P + 1 + t_glob).astype(F32)
            y = s[POOL_HALO:, :] / count - u[:, cs]
            groups.append(jnp.dot(y.astype(BF16), w_ref[g], preferred_element_type=F32))
        per_seq.append(jnp.concatenate(groups, axis=1) * sc_ref[...])
    y = per_seq[0] if nb == 1 else jnp.concatenate(per_seq, axis=0)
    y_ref[...] = y.astype(y_ref.dtype)


def _pool(z, prefix, w_pool, pool_scale, ybuf, row0, B, T, Tt, off_u):
    G, Cg, _ = w_pool.shape
    PD = G * Cg
    nt = T // Tt
    nb = 1 if Tt % BF16_SUBLANES == 0 else BF16_SUBLANES // Tt
    assert nb == 1 or nt == 1
    R = nb * Tt
    assert B % nb == 0 and row0 % R == 0 and off_u % PD == 0
    assert Tt % POOL_HALO == 0 or prefix is not None
    rb0 = row0 // R
    if prefix is None:
        hb = Tt // POOL_HALO
        halo_spec = pl.BlockSpec((POOL_HALO, PD),
                                 lambda b, n: (jnp.maximum((rb0 + b * nt + n) * hb - 1, 0), off_u // PD))
        halo_arg, P = z, 0
    else:
        halo_spec = pl.BlockSpec((nb, POOL_HALO, PD), lambda b, n: (b, 0, 0))
        halo_arg, P = prefix, POOL_HALO - 1
    return pl.pallas_call(
        functools.partial(_pool_kernel, Tt=Tt, nb=nb, P=P, G=G, Cg=Cg,
                          first_block_has_no_past=prefix is None),
        grid=(B // nb, nt),
        in_specs=[pl.BlockSpec((R, PD), lambda b, n: (rb0 + b * nt + n, off_u // PD)),
                  halo_spec,
                  pl.BlockSpec((G, Cg, Cg), lambda b, n: (0, 0, 0)),
                  pl.BlockSpec((1, PD), lambda b, n: (0, 0)),
                  pl.BlockSpec(memory_space=pl.ANY)],
        out_specs=pl.BlockSpec((R, PD), lambda b, n: (rb0 + b * nt + n, 0)),
        out_shape=jax.ShapeDtypeStruct(ybuf.shape, ybuf.dtype),
        input_output_aliases={4: 0},
        compiler_params=_params("parallel", "arbitrary"),
        name="pool",
    )(z, halo_arg, w_pool, pool_scale, ybuf)


def _plain(dot, extras):
    return dot(0)


def _gate_combine(dot, extras):
    acc = jax.nn.sigmoid(dot(0)) * dot(1)
    acc = acc + jax.nn.sigmoid(dot(2)) * dot(3)
    return acc + jax.nn.sigmoid(dot(4)) * dot(5)


def _residual(dot, extras):
    return extras[0][...] + dot(0)


def _ple(dot, extras):
    return extras[0][...] + dot(1) * jax.nn.sigmoid(dot(0))


def kernel(x_prompt, x_sample, state_gla, cache_swa_k, cache_swa_v, state_pool, state_ffn_conv, p_prompt, p_sample, g_mix, w_in, w_gla_a2, b_gla_a, g_gla_out, swa_sinks, w_pool, pool_scale, w_branch_gla, w_branch_swa, w_branch_pool, w_gate, w_out, g_ffn, w_up, w_conv, b_conv, w_down, g_ple, w_ple, w_ple_gate, g_final):
    Bp, Tp, D = x_prompt.shape
    Bs, Ts, _ = x_sample.shape
    depth = w_in.shape[0]
    _, _, H, DK, DV = state_gla.shape
    rank = w_gla_a2.shape[1]
    _, _, WB, KV, dh = cache_swa_k.shape
    NH = swa_sinks.shape[1]
    G = NH // KV
    _, PG, Cg, _ = w_pool.shape
    PD = PG * Cg
    P = state_pool.shape[2]
    F = w_down.shape[1]
    CW = w_conv.shape[1]
    assert WB == WINDOW and P == POOL_HALO - 1 and CW == 3 and PG == len(POOL_WINDOWS)
    assert Tp >= WINDOW and Tp >= P and Ts >= CW - 1
    GK, GV, SQ, SKV = H * DK, H * DV, NH * dh, KV * dh
    Mp, Ms = Bp * Tp, Bs * Ts
    M = Mp + Ms

    off_q, off_k, off_v, off_r = 0, GK, 2 * GK, 2 * GK + GV
    off_u = 2 * GK + 2 * GV
    off_sq = off_u + PD
    off_sk = off_sq + SQ
    off_sv = off_sk + SKV
    NZ = off_sv + SKV
    in_sizes = (GK, GK, GV, GV, rank, SQ, SKV, SKV, PD)
    in_off = [0]
    for s in in_sizes:
        in_off.append(in_off[-1] + s)
    segments = ((0, in_off[4]), (in_off[8], in_off[9]), (in_off[5], in_off[8]))

    tm = _pick_tile(M, 1056, BF16_SUBLANES)
    tm_half = _pick_tile(M, 528, BF16_SUBLANES)
    tf = _pick_tile(F, 256, LANES)
    tk_down = _pick_tile(F, 5504, LANES)
    tn_z = _pick_tile(NZ, 768, LANES)
    tn_d = _pick_tile(D, 512, LANES)
    tn_g = _pick_tile(D, 256, LANES)
    Tt_pool = _pick_tile(Tp, 256, POOL_HALO)
    M_tail = Bp * SUBLANES + Ms

    w_main_b, w_lr_b = _cast_in_proj(jnp.swapaxes(w_in, 1, 2), segments, (in_off[4], in_off[5]))
    w_gate_b = _cast_weights(w_gate)
    w_br_b = [_cast_weights(w_branch_gla), _cast_weights(w_branch_swa), _cast_weights(w_branch_pool)]
    w_out_b = _cast_weights(w_out)
    w_up_b = _cast_weights(w_up)
    w_down_b = _cast_weights(w_down)
    w_pg_b = _cast_weights(w_ple_gate)
    w_ple_b = w_ple.astype(BF16)
    w_a2_b = w_gla_a2.astype(BF16)
    w_pool_b = w_pool.astype(BF16)
    b_conv3 = b_conv.reshape(depth, 1, F)

    heads = jnp.arange(1, NH + 1, dtype=F32)
    slopes = jnp.exp2(-8.0 * heads / NH).reshape(KV, G, 1)

    qi = jnp.arange(WINDOW, dtype=jnp.int32)[:, None]
    kj = jnp.arange(2 * WINDOW, dtype=jnp.int32)[None, :]
    dist = qi + WINDOW - kj
    in_win = (dist >= 0) & (dist < WINDOW)
    has_key = jnp.stack([in_win & (kj >= WINDOW), in_win])
    swa_bias = jnp.where(has_key[:, None], -(slopes.reshape(NH, 1, 1) * dist.astype(F32))[None], -jnp.inf)

    def head_cols(v, tq):
        return jnp.broadcast_to(v[:, :, None, :], (KV, G, tq, 1)).reshape(KV, G * tq, 1)

    x = jnp.concatenate([x_prompt.reshape(Mp, D), x_sample.reshape(Ms, D)], axis=0)
    p_all = jnp.concatenate([p_prompt.reshape(depth, Mp, -1), p_sample.reshape(depth, Ms, -1)],
                            axis=1).astype(BF16)
    gla_p = jnp.zeros((depth, Bp, H, DK, DV), F32)
    gla_s = jnp.zeros((depth, Bs, H, DK, DV), F32)
    ya = jnp.zeros((M, GV), BF16)
    yb = jnp.zeros((M, SQ), BF16)
    yc = jnp.zeros((M, PD), BF16)
    tail_rows = [slice((b + 1) * Tp - SUBLANES, (b + 1) * Tp) for b in range(Bp)]
    outs_p, outs_s = [], []

    for i in range(depth):
        b_a = b_gla_a[i].reshape(1, GK)
        g_o = g_gla_out[i].reshape(1, GV)
        p_scale = pool_scale[i].reshape(1, PD)
        sinks = swa_sinks[i].reshape(KV, G, 1)

        h = _rmsnorm(x, g_mix[i], BF16)
        z = _fused_matmul([h], [(0, w_main_b, 0)], [], _plain, i, NZ, F32, tm, tn_z, "in_proj")
        zlr = _fused_matmul([h], [(0, w_lr_b, 0)], [], _plain, i, LANES, F32, tm, LANES, "in_proj_lr")

        gla_args = (w_a2_b[i], b_a, g_o)
        gla_dims = (H, DK, DV, off_q, off_k, off_v, off_r)
        ya, gla_p = _gla(z, zlr, *gla_args, None, ya, gla_p, i, 0, Bp, Tp, *gla_dims)
        ya, gla_s = _gla(z, zlr, *gla_args, state_gla, ya, gla_s, i, Mp, Bs, Ts, *gla_dims)

        swa_dims = (KV, G, dh, off_sq, off_sk, off_sv)
        yb = _swa_blocks(z, swa_bias, swa_sinks[i], yb, Bp, Tp, *swa_dims)
        ck = cache_swa_k[i].reshape(Bs, WB, SKV)
        cv = cache_swa_v[i].reshape(Bs, WB, SKV)
        yb = _swa(z, ck, cv, head_cols(slopes, Ts), head_cols(sinks, Ts), yb, Mp, Bs, Ts, Ts, *swa_dims)

        yc = _pool(z, None, w_pool_b[i], p_scale, yc, 0, Bp, Tp, Tt_pool, off_u)
        pool_prefix = jnp.pad(state_pool[i], ((0, 0), (POOL_HALO - P, 0), (0, 0)))
        yc = _pool(z, pool_prefix, w_pool_b[i], p_scale, yc, Mp, Bs, Ts, Ts, off_u)

        m = _fused_matmul([h, ya, yb, yc],
                          [(0, w_gate_b, 0), (1, w_br_b[0], 0), (0, w_gate_b, D), (2, w_br_b[1], 0),
                           (0, w_gate_b, 2 * D), (3, w_br_b[2], 0)],
                          [], _gate_combine, i, D, BF16, tm_half, tn_g, "gate_combine")
        x = _fused_matmul([m], [(0, w_out_b, 0)], [x], _residual, i, D, F32, tm, tn_d, "out_proj")

        h = _rmsnorm(x, g_ffn[i], BF16)
        pre = state_ffn_conv[i]
        r0 = jnp.broadcast_to(pre[:, 0:1], (Bs, Ts, F)).reshape(Ms, F)
        r1 = jnp.broadcast_to(pre[:, 1:2], (Bs, Ts, F)).reshape(Ms, F)
        act = _up_conv(h, w_up_b, w_conv, b_conv3, r0, r1, i, Mp, Tp, Ts, tm, tf)
        x = _residual_matmul_ksplit(act, w_down_b, i, x, tm, tn_d, tk_down, "down_proj")
        h_tail = jnp.concatenate([h[r] for r in tail_rows] + [h[Mp:]], axis=0)
        a_tail = _fused_matmul([h_tail], [(0, w_up_b, 0)], [], _plain, i, F, F32, M_tail, tf, "up_proj_tail")

        h = _rmsnorm(x, g_ple[i], BF16)
        x = _fused_matmul([h, p_all[i]], [(0, w_pg_b, 0), (1, w_ple_b, 0)], [x], _ple, i, D, F32,
                          tm, tn_d, "ple")

        wb = min(WINDOW, Tp)

        def prompt_tail(n_rows, off, width):
            return jnp.stack([z[(b + 1) * Tp - n_rows:(b + 1) * Tp, off:off + width] for b in range(Bp)])

        zs = z[Mp:].reshape(Bs, Ts, NZ)
        swa_k_p = prompt_tail(wb, off_sk, SKV).reshape(Bp, wb, KV, dh)
        swa_v_p = prompt_tail(wb, off_sv, SKV).reshape(Bp, wb, KV, dh)
        pool_p = prompt_tail(P, off_u, PD)
        conv_p = a_tail[:Bp * SUBLANES].reshape(Bp, SUBLANES, F)[:, SUBLANES - (CW - 1):]
        swa_k_s = jnp.concatenate([ck, zs[:, :, off_sk:off_sk + SKV]], axis=1)[:, Ts:].reshape(Bs, WB, KV, dh)
        swa_v_s = jnp.concatenate([cv, zs[:, :, off_sv:off_sv + SKV]], axis=1)[:, Ts:].reshape(Bs, WB, KV, dh)
        pool_s = jnp.concatenate([state_pool[i], zs[:, :, off_u:off_u + PD]], axis=1)[:, Ts:]
        conv_s = a_tail[Bp * SUBLANES:].reshape(Bs, Ts, F)[:, Ts - (CW - 1):]
        outs_p.append((swa_k_p, swa_v_p, pool_p, conv_p))
        outs_s.append((swa_k_s, swa_v_s, pool_s, conv_s))

    y_prompt = _rmsnorm(x, g_final, F32, 0, Mp).reshape(Bp, Tp, D)
    y_sample = _rmsnorm(x, g_final, F32, Mp, Ms).reshape(Bs, Ts, D)
    stack = lambda outs, c: jnp.stack([o[c] for o in outs])
    return (y_prompt, y_sample,
            gla_p, stack(outs_p, 0), stack(outs_p, 1), stack(outs_p, 2), stack(outs_p, 3),
            gla_s, stack(outs_s, 0), stack(outs_s, 1), stack(outs_s, 2), stack(outs_s, 3))
```

```python
import functools
import math

import jax
import jax.numpy as jnp
from jax import lax
from jax.experimental import pallas as pl
from jax.experimental.pallas import tpu as pltpu

EPS = 1e-6
GLA_TAU = 16.0
GLA_CHUNK = 64
WINDOW = 128
POOL_WINDOWS = (2, 4, 8, 16)

VMEM_LIMIT_BYTES = 56 * 1024 * 1024
LANES = 128
SUBLANES = 8
BF16_SUBLANES = 16
POOL_HALO = 16
CAST_BLOCK_BYTES = 6 * 1024 * 1024

F32 = jnp.float32
BF16 = jnp.bfloat16


def _pick_tile(n, target, align):
    best = None
    for t in range(align, min(n, target) + 1, align):
        if n % t == 0:
            best = t
    return n if best is None else best


def _params(*sem):
    return pltpu.CompilerParams(dimension_semantics=sem, vmem_limit_bytes=VMEM_LIMIT_BYTES)


def _cast_kernel(w_ref, o_ref):
    o_ref[...] = w_ref[...].astype(o_ref.dtype)


def _cast_weights(w, layer=None):
    L, K, N = w.shape
    tk = _pick_tile(K, max(BF16_SUBLANES, CAST_BLOCK_BYTES // (4 * N)), BF16_SUBLANES)
    if layer is None:
        spec = pl.BlockSpec((None, tk, N), lambda l, k: (l, k, 0))
        return pl.pallas_call(
            _cast_kernel, grid=(L, K // tk), in_specs=[spec], out_specs=spec,
            out_shape=jax.ShapeDtypeStruct((L, K, N), BF16),
            compiler_params=_params("parallel", "parallel"), name="cast_weights",
        )(w)
    return pl.pallas_call(
        _cast_kernel, grid=(K // tk,),
        in_specs=[pl.BlockSpec((None, tk, N), lambda k: (layer, k, 0))],
        out_specs=pl.BlockSpec((tk, N), lambda k: (k, 0)),
        out_shape=jax.ShapeDtypeStruct((K, N), BF16),
        compiler_params=_params("parallel"), name="cast_layer",
    )(w)


def _side_cast(src, layer, steps, step_of):
    _, K, N = src.shape
    rows = next(r for r in range(BF16_SUBLANES, K + 1, BF16_SUBLANES) if K % r == 0 and K // r <= steps)
    last = K // rows - 1
    in_spec = pl.BlockSpec((None, rows, N), lambda *g: (layer, jnp.minimum(step_of(*g), last), 0))
    out_spec = pl.BlockSpec((rows, N), lambda *g: (jnp.minimum(step_of(*g), last), 0))
    return in_spec, out_spec, jax.ShapeDtypeStruct((K, N), BF16)


def _weight_spec(w, layer, rows, cols, index):
    if w.ndim == 2:
        return pl.BlockSpec((rows, cols), index)
    return pl.BlockSpec((None, rows, cols), lambda *g: (layer,) + tuple(index(*g)))


def _cast_in_proj_kernel(wt_ref, main_ref, lr_ref, *, segments, lr):
    wt = wt_ref[...]
    main = jnp.concatenate([wt[a:b, :] for a, b in segments], axis=0)
    main_ref[...] = main.T.astype(main_ref.dtype)
    pad = jnp.zeros((lr_ref.shape[1] - (lr[1] - lr[0]), wt.shape[1]), F32)
    lr_ref[...] = jnp.concatenate([wt[lr[0]:lr[1], :], pad], axis=0).T.astype(lr_ref.dtype)


def _cast_in_proj(wt, segments, lr):
    L, N, K = wt.shape
    n_main = sum(b - a for a, b in segments)
    tk = _pick_tile(K, LANES, LANES)
    return pl.pallas_call(
        functools.partial(_cast_in_proj_kernel, segments=segments, lr=lr),
        grid=(L, K // tk),
        in_specs=[pl.BlockSpec((None, N, tk), lambda l, k: (l, 0, k))],
        out_specs=[pl.BlockSpec((None, tk, n_main), lambda l, k: (l, k, 0)),
                   pl.BlockSpec((None, tk, LANES), lambda l, k: (l, k, 0))],
        out_shape=[jax.ShapeDtypeStruct((L, K, n_main), BF16),
                   jax.ShapeDtypeStruct((L, K, LANES), BF16)],
        compiler_params=_params("parallel", "parallel"), name="cast_in_proj",
    )(wt)


def _rmsnorm_kernel(x_ref, g_ref, o_ref):
    x = x_ref[...]
    y = x * lax.rsqrt(jnp.mean(x * x, axis=-1, keepdims=True) + EPS)
    o_ref[...] = (y * g_ref[...]).astype(o_ref.dtype)


def _rmsnorm(x, g, out_dtype, row0=0, rows=None):
    M, D = x.shape
    rows = M if rows is None else rows
    tm = _pick_tile(math.gcd(rows, row0) if row0 else rows, 512, BF16_SUBLANES)
    rb0 = row0 // tm
    return pl.pallas_call(
        _rmsnorm_kernel,
        grid=(rows // tm,),
        in_specs=[pl.BlockSpec((tm, D), lambda i: (rb0 + i, 0)),
                  pl.BlockSpec((1, D), lambda i: (0, 0))],
        out_specs=pl.BlockSpec((tm, D), lambda i: (i, 0)),
        out_shape=jax.ShapeDtypeStruct((rows, D), out_dtype),
        compiler_params=_params("parallel"),
        name="rmsnorm",
    )(x, g.reshape(1, D))


def _mm_kernel(*refs, n_acts, pair_act, n_extra, combine, has_cast):
    n_pairs = len(pair_act)
    n_in = n_acts + n_pairs + n_extra
    w_refs = refs[n_acts:n_acts + n_pairs]
    extras = refs[n_acts + n_pairs:n_in]
    o_ref = refs[n_in + has_cast]

    def dot(p):
        return jnp.dot(refs[pair_act[p]][...], w_refs[p][...], preferred_element_type=F32)

    o_ref[...] = combine(dot, extras).astype(o_ref.dtype)
    if has_cast:
        refs[n_in + 2][...] = refs[n_in][...].astype(BF16)


def _fused_matmul(acts, pairs, extras, combine, n_out, out_dtype, tm, tn, name, cast=None):
    M = acts[0].shape[0]
    n_j = n_out // tn
    in_specs = [pl.BlockSpec((tm, a.shape[1]), lambda i, j: (i, 0)) for a in acts]
    args = list(acts)
    for ai, w, layer, off in pairs:
        K = acts[ai].shape[1]
        assert off % tn == 0 and w.shape[-2] == K
        in_specs.append(_weight_spec(w, layer, K, tn, lambda i, j, ob=off // tn: (0, j + ob)))
        args.append(w)
    for e in extras:
        in_specs.append(pl.BlockSpec((tm, tn), lambda i, j: (i, j)))
        args.append(e)
    out_specs = [pl.BlockSpec((tm, tn), lambda i, j: (i, j))]
    out_shape = [jax.ShapeDtypeStruct((M, n_out), out_dtype)]
    if cast is not None:
        c_in, c_out, c_shape = _side_cast(cast[0], cast[1], (M // tm) * n_j, lambda i, j: i * n_j + j)
        in_specs.append(c_in)
        args.append(cast[0])
        out_specs.append(c_out)
        out_shape.append(c_shape)
    res = pl.pallas_call(
        functools.partial(_mm_kernel, n_acts=len(acts), pair_act=tuple(p[0] for p in pairs),
                          n_extra=len(extras), combine=combine, has_cast=cast is not None),
        grid=(M // tm, n_j),
        in_specs=in_specs,
        out_specs=out_specs,
        out_shape=out_shape,
        compiler_params=_params("arbitrary", "arbitrary"),
        name=name,
    )(*args)
    return res[0] if cast is None else res


def _mm_ksplit_kernel(*refs, nk, has_cast):
    a_ref, w_ref, x_ref = refs[:3]
    o_ref, acc_ref = refs[3 + has_cast], refs[-1]
    if has_cast:
        refs[5][...] = refs[3][...].astype(BF16)
    k = pl.program_id(2)
    d = jnp.dot(a_ref[...], w_ref[...], preferred_element_type=F32)

    @pl.when(k == 0)
    def _():
        acc_ref[...] = d

    @pl.when(k > 0)
    def _():
        acc_ref[...] += d

    @pl.when(k == nk - 1)
    def _():
        o_ref[...] = x_ref[...] + acc_ref[...]


def _residual_matmul_ksplit(a, w, x, tm, tn, tk, name, cast=None):
    M, K = a.shape
    N = w.shape[1]
    nk, n_j = K // tk, N // tn
    in_specs = [pl.BlockSpec((tm, tk), lambda i, j, k: (i, k)),
                pl.BlockSpec((tk, tn), lambda i, j, k: (k, j)),
                pl.BlockSpec((tm, tn), lambda i, j, k: (i, j))]
    out_specs = [pl.BlockSpec((tm, tn), lambda i, j, k: (i, j))]
    out_shape = [jax.ShapeDtypeStruct((M, N), F32)]
    args = [a, w, x]
    if cast is not None:
        c_in, c_out, c_shape = _side_cast(cast[0], cast[1], (M // tm) * n_j * nk,
                                          lambda i, j, k: (i * n_j + j) * nk + k)
        in_specs.append(c_in)
        args.append(cast[0])
        out_specs.append(c_out)
        out_shape.append(c_shape)
    res = pl.pallas_call(
        functools.partial(_mm_ksplit_kernel, nk=nk, has_cast=cast is not None),
        grid=(M // tm, n_j, nk),
        in_specs=in_specs,
        out_specs=out_specs,
        out_shape=out_shape,
        scratch_shapes=[pltpu.VMEM((tm, tn), F32)],
        compiler_params=_params("arbitrary", "arbitrary", "arbitrary"),
        name=name,
    )(*args)
    return res[0] if cast is None else res


GELU_K0 = -2.0 * math.sqrt(2.0 / math.pi) * math.log2(math.e)
GELU_K1 = 0.044715 * GELU_K0


def _conv_gelu(a, a1, a2, wc_ref, bc_ref):
    x = bc_ref[...] + a2 * wc_ref[0:1, :]
    x = x + a1 * wc_ref[1:2, :]
    x = x + a * wc_ref[2:3, :]
    return x / (1.0 + jnp.exp2(x * (GELU_K0 + GELU_K1 * (x * x))))


def _up_conv_kernel(*refs, tm, Mp, Tp, Ts, Ms, piece, has_cast):
    h_ref, wa_ref, wg_ref, wc_ref, bc_ref, r0_ref, r1_ref = refs[:7]
    o_ref = refs[7 + has_cast]
    a_scr, g_scr, halo_scr = refs[-3:]
    if has_cast:
        refs[9][...] = refs[7][...].astype(BF16)
    i, j = pl.program_id(0), pl.program_id(1)

    @pl.when(i == 0)
    def _():
        halo_scr[j] = jnp.zeros(halo_scr.shape[1:], F32)

    h = h_ref[...]
    a_scr[0:SUBLANES, :] = halo_scr[j]
    a_scr[SUBLANES:, :] = jnp.dot(h, wa_ref[...], preferred_element_type=F32)
    g_scr[...] = jnp.dot(h, wg_ref[...], preferred_element_type=F32)
    halo_scr[j] = a_scr[tm:, :]

    row0 = i * tm
    sub = lax.broadcasted_iota(jnp.int32, (SUBLANES, 1), 0)
    for lo in range(0, tm, piece):
        ext = a_scr[lo:lo + SUBLANES + piece, :]
        a = ext[SUBLANES:, :]
        a1 = pltpu.roll(ext, 1, 0)[SUBLANES:, :]
        a2 = pltpu.roll(ext, 2, 0)[SUBLANES:, :]
        if lo < tm - Ms:
            starts = (row0 + lo) % Tp == 0
            top1 = jnp.where(starts & (sub == 0), 0.0, a1[:SUBLANES, :])
            top2 = jnp.where(starts & (sub < 2), 0.0, a2[:SUBLANES, :])
            a1 = jnp.concatenate([top1, a1[SUBLANES:, :]], axis=0)
            a2 = jnp.concatenate([top2, a2[SUBLANES:, :]], axis=0)
        else:
            row = row0 + lo + lax.broadcasted_iota(jnp.int32, (piece, 1), 0)
            is_sample = row >= Mp
            pos = jnp.where(is_sample, (row - Mp) % Ts, row % Tp)
            ls = lo - (tm - Ms)
            hist2 = jnp.where(is_sample, r0_ref[ls:ls + piece, :], 0.0)
            hist1 = jnp.where(is_sample, r1_ref[ls:ls + piece, :], 0.0)
            a1 = jnp.where(pos == 0, hist1, a1)
            a2 = jnp.where(pos == 0, hist2, jnp.where(pos == 1, hist1, a2))
        act = _conv_gelu(a, a1, a2, wc_ref, bc_ref) * g_scr[lo:lo + piece, :]
        o_ref[lo:lo + piece, :] = act.astype(o_ref.dtype)


def _up_conv(h, w_up, w_conv, b_conv, r0, r1, layer, Mp, Tp, Ts, tm, tf, cast=None):
    M, K = h.shape
    F = w_conv.shape[2]
    Ms = r0.shape[0]
    n_i, n_f = M // tm, F // tf
    piece = next(p for p in (2 * BF16_SUBLANES, BF16_SUBLANES) if tm % p == 0 and (tm - Ms) % p == 0)
    assert M % tm == 0 and F % tf == 0 and M - Mp == Ms and Ms <= tm
    assert tm % piece == 0 and (tm - Ms) % piece == 0 and Tp % piece == 0
    in_specs = [pl.BlockSpec((tm, K), lambda i, j: (i, 0)),
                pl.BlockSpec((K, tf), lambda i, j: (0, j)),
                pl.BlockSpec((K, tf), lambda i, j: (0, n_f + j)),
                pl.BlockSpec((None, w_conv.shape[1], tf), lambda i, j: (layer, 0, j)),
                pl.BlockSpec((None, 1, tf), lambda i, j: (layer, 0, j)),
                pl.BlockSpec((Ms, tf), lambda i, j: (0, j)),
                pl.BlockSpec((Ms, tf), lambda i, j: (0, j))]
    out_specs = [pl.BlockSpec((tm, tf), lambda i, j: (i, j))]
    out_shape = [jax.ShapeDtypeStruct((M, F), BF16)]
    args = [h, w_up, w_up, w_conv, b_conv, r0, r1]
    if cast is not None:
        c_in, c_out, c_shape = _side_cast(cast[0], cast[1], n_i * n_f, lambda i, j: i * n_f + j)
        in_specs.append(c_in)
        args.append(cast[0])
        out_specs.append(c_out)
        out_shape.append(c_shape)
    res = pl.pallas_call(
        functools.partial(_up_conv_kernel, tm=tm, Mp=Mp, Tp=Tp, Ts=Ts, Ms=Ms, piece=piece,
                          has_cast=cast is not None),
        grid=(n_i, n_f),
        in_specs=in_specs,
        out_specs=out_specs,
        out_shape=out_shape,
        scratch_shapes=[pltpu.VMEM((SUBLANES + tm, tf), F32), pltpu.VMEM((tm, tf), F32),
                        pltpu.VMEM((n_f, SUBLANES, tf), F32)],
        compiler_params=_params("arbitrary", "arbitrary"),
        name="up_conv",
    )(*args)
    return res[0] if cast is None else res


def _gla_chunk(q, k, v, glr, w_a2, b_a, S, C):
    DK = q.shape[1]
    la = jnp.dot(glr.astype(BF16), w_a2, preferred_element_type=F32) + b_a
    log_a = (jnp.minimum(la, 0.0) - jnp.log1p(jnp.exp(-jnp.abs(la)))) / GLA_TAU
    row = lax.broadcasted_iota(jnp.int32, (C, DK), 0)
    b = log_a
    s = 1
    while s < C:
        b = b + jnp.where(row >= s, pltpu.roll(b, s, 0), 0.0)
        s *= 2
    b_last = b[C - 1:C, :]
    q_t = ((q * (DK ** -0.5)) * jnp.exp(b)).astype(BF16)
    k_t = (k * jnp.exp(-b)).astype(BF16)
    att = lax.dot_general(q_t, k_t, (((1,), (1,)), ((), ())), preferred_element_type=F32)
    ri = lax.broadcasted_iota(jnp.int32, (C, C), 0)
    ci = lax.broadcasted_iota(jnp.int32, (C, C), 1)
    att = jnp.where(ri >= ci, att, 0.0)
    vb = v.astype(BF16)
    o = (jnp.dot(att.astype(BF16), vb, preferred_element_type=F32)
         + jnp.dot(q_t, S.astype(BF16), preferred_element_type=F32))
    k_end = (k * jnp.exp(b_last - b)).astype(BF16)
    er = lax.broadcasted_iota(jnp.int32, (DK, DK), 0)
    ec = lax.broadcasted_iota(jnp.int32, (DK, DK), 1)
    decay_col = jnp.sum(jnp.where(er == ec, jnp.exp(b_last), 0.0), axis=1, keepdims=True)
    S_new = decay_col * S + lax.dot_general(k_end, vb, (((0,), (0,)), ((), ())),
                                            preferred_element_type=F32)
    return o, S_new


def _gla_kernel(*refs, C, nb, H, DK, DV, rank, n_chunks, has_s0):
    q_ref, k_ref, v_ref, r_ref, lr_ref, wa_ref, ba_ref, go_ref = refs[:8]
    s0_ref = refs[8] if has_s0 else None
    y_ref, sout_ref, s_scr = refs[-3:]
    n = pl.program_id(1)

    @pl.when(n == 0)
    def _():
        s_scr[...] = s0_ref[...] if has_s0 else jnp.zeros(s_scr.shape, F32)

    per_seq = []
    for j in range(nb):
        rows = slice(j * C, (j + 1) * C)
        glr = lr_ref[rows, :rank]
        heads = []
        for h in range(H):
            kc, vc = slice(h * DK, (h + 1) * DK), slice(h * DV, (h + 1) * DV)
            o, s_new = _gla_chunk(q_ref[rows, kc], k_ref[rows, kc], v_ref[rows, vc], glr,
                                  wa_ref[:, kc], ba_ref[:, kc], s_scr[j, h], C)
            s_scr[j, h] = s_new
            o = o * lax.rsqrt(jnp.mean(o * o, axis=-1, keepdims=True) + EPS) * go_ref[:, vc]
            r = r_ref[rows, vc]
            heads.append(o * (r * jax.nn.sigmoid(r)))
        per_seq.append(jnp.concatenate(heads, axis=1))
    y = per_seq[0] if nb == 1 else jnp.concatenate(per_seq, axis=0)
    y_ref[...] = y.astype(y_ref.dtype)

    @pl.when(n == n_chunks - 1)
    def _():
        sout_ref[...] = s_scr[...]


def _gla(z, zlr, w_a2, b_a, g_out, s0, ybuf, sbuf, layer, row0, B, T, H, DK, DV, off_q, off_k, off_v, off_r):
    rank = w_a2.shape[0]
    GK, GV = H * DK, H * DV
    C = math.gcd(T, GLA_CHUNK)
    n_chunks = T // C
    nb = 1 if C % BF16_SUBLANES == 0 else BF16_SUBLANES // C
    assert nb == 1 or n_chunks == 1
    assert B % nb == 0 and row0 % (nb * C) == 0
    assert off_q % GK == 0 and off_k % GK == 0 and off_v % GV == 0 and off_r % GV == 0
    R = nb * C
    rb0 = row0 // R

    def zspec(width, off):
        return pl.BlockSpec((R, width), lambda b, n: (rb0 + b * n_chunks + n, off // width))

    state_spec = pl.BlockSpec((None, nb, H, DK, DV), lambda b, n: (layer, b, 0, 0, 0))
    full = lambda a: pl.BlockSpec(a.shape, lambda b, n: (0,) * a.ndim)
    any_spec = pl.BlockSpec(memory_space=pl.ANY)
    in_specs = [zspec(GK, off_q), zspec(GK, off_k), zspec(GV, off_v), zspec(GV, off_r),
                pl.BlockSpec((R, zlr.shape[1]), lambda b, n: (rb0 + b * n_chunks + n, 0)),
                full(w_a2), full(b_a), full(g_out)]
    args = [z, z, z, z, zlr, w_a2, b_a, g_out]
    if s0 is not None:
        in_specs.append(state_spec)
        args.append(s0)
    n_in = len(args)
    return pl.pallas_call(
        functools.partial(_gla_kernel, C=C, nb=nb, H=H, DK=DK, DV=DV, rank=rank, n_chunks=n_chunks,
                          has_s0=s0 is not None),
        grid=(B // nb, n_chunks),
        in_specs=in_specs + [any_spec, any_spec],
        out_specs=[pl.BlockSpec((R, GV), lambda b, n: (rb0 + b * n_chunks + n, 0)), state_spec],
        out_shape=[jax.ShapeDtypeStruct(ybuf.shape, ybuf.dtype),
                   jax.ShapeDtypeStruct(sbuf.shape, sbuf.dtype)],
        scratch_shapes=[pltpu.VMEM((nb, H, DK, DV), F32)],
        input_output_aliases={n_in: 0, n_in + 1: 1},
        compiler_params=_params("parallel", "arbitrary"),
        name="gla",
    )(*args, ybuf, sbuf)


def _swa_block_kernel(q_ref, k_ref, v_ref, kp_ref, vp_ref, bias_ref, sink_ref, ybuf_ref, y_ref,
                      *, KV, G, dh):
    del ybuf_ref
    scale = dh ** -0.5
    fold_scale = math.frexp(scale)[0] == 0.5
    heads = []
    for kv in range(KV):
        cs = slice(kv * dh, (kv + 1) * dh)
        kk = jnp.concatenate([kp_ref[:, cs], k_ref[:, cs]], axis=0).astype(BF16)
        vv = jnp.concatenate([vp_ref[:, cs], v_ref[:, cs]], axis=0).astype(BF16)
        for g in range(G):
            h = kv * G + g
            q = q_ref[:, h * dh:(h + 1) * dh]
            q = (q * scale if fold_scale else q).astype(BF16)
            s = lax.dot_general(q, kk, (((1,), (1,)), ((), ())), preferred_element_type=F32)
            s = (s if fold_scale else s * scale) + bias_ref[h]
            sink = sink_ref[h]
            m = jnp.maximum(jnp.max(s, axis=1, keepdims=True), sink)
            e = jnp.exp(s - m)
            denom = jnp.sum(e, axis=1, keepdims=True) + jnp.exp(sink - m)
            heads.append(jnp.dot((e / denom).astype(BF16), vv, preferred_element_type=F32))
    y_ref[...] = jnp.concatenate(heads, axis=1).astype(y_ref.dtype)


def _swa_blocks(z, bias, sinks, ybuf, B, T, KV, G, dh, off_q, off_k, off_v):
    nq = T // WINDOW
    QW, KW = KV * G * dh, KV * dh
    assert T % WINDOW == 0 and off_q % QW == 0 and off_k % KW == 0 and off_v % KW == 0

    def own(width, off):
        return pl.BlockSpec((WINDOW, width), lambda b, n: (b * nq + n, off // width))

    def past(off):
        return pl.BlockSpec((WINDOW, KW), lambda b, n: (jnp.maximum(b * nq + n - 1, 0), off // KW))

    return pl.pallas_call(
        functools.partial(_swa_block_kernel, KV=KV, G=G, dh=dh),
        grid=(B, nq),
        in_specs=[own(QW, off_q), own(KW, off_k), own(KW, off_v), past(off_k), past(off_v),
                  pl.BlockSpec((None,) + bias.shape[1:], lambda b, n: (jnp.minimum(n, 1), 0, 0, 0)),
                  pl.BlockSpec(memory_space=pltpu.SMEM),
                  pl.BlockSpec(memory_space=pl.ANY)],
        out_specs=pl.BlockSpec((WINDOW, QW), lambda b, n: (b * nq + n, 0)),
        out_shape=jax.ShapeDtypeStruct(ybuf.shape, ybuf.dtype),
        input_output_aliases={7: 0},
        compiler_params=_params("parallel", "arbitrary"),
        name="swa_blocks",
    )(z, z, z, z, z, bias, sinks, ybuf)


def _swa_kernel(q_ref, k_ref, v_ref, kp_ref, vp_ref, slope_ref, sink_ref, ybuf_ref, y_ref,
                *, Tq, nb, KV, G, dh, first_block_has_no_past):
    del ybuf_ref
    n = pl.program_id(1)
    W = WINDOW
    rows, keys = G * Tq, W + Tq
    ri = lax.broadcasted_iota(jnp.int32, (rows, keys), 0)
    ci = lax.broadcasted_iota(jnp.int32, (rows, keys), 1)
    dist = (ri % Tq) + W - ci
    valid = (dist >= 0) & (dist < WINDOW)
    if first_block_has_no_past:
        valid = valid & ((ci >= W) | (n > 0))
    distf = dist.astype(F32)
    per_seq = []
    for j in range(nb):
        rs = slice(j * Tq, (j + 1) * Tq)
        q, k_own, v_own = q_ref[rs, :], k_ref[rs, :], v_ref[rs, :]
        k_past = kp_ref[j] if len(kp_ref.shape) == 3 else kp_ref[...]
        v_past = vp_ref[j] if len(vp_ref.shape) == 3 else vp_ref[...]
        heads = []
        for kv in range(KV):
            cs = slice(kv * dh, (kv + 1) * dh)
            kk = jnp.concatenate([k_past[:, cs], k_own[:, cs]], axis=0).astype(BF16)
            vv = jnp.concatenate([v_past[:, cs], v_own[:, cs]], axis=0).astype(BF16)
            qs = jnp.concatenate(
                [q[:, (kv * G + g) * dh:(kv * G + g + 1) * dh] for g in range(G)], axis=0).astype(BF16)
            s = lax.dot_general(qs, kk, (((1,), (1,)), ((), ())), preferred_element_type=F32)
            s = s * (dh ** -0.5) - slope_ref[kv] * distf
            s = jnp.where(valid, s, -jnp.inf)
            sink = sink_ref[kv]
            m = jnp.maximum(jnp.max(s, axis=1, keepdims=True), sink)
            e = jnp.exp(s - m)
            denom = jnp.sum(e, axis=1, keepdims=True) + jnp.exp(sink - m)
            o = jnp.dot((e / denom).astype(BF16), vv, preferred_element_type=F32)
            heads += [o[g * Tq:(g + 1) * Tq, :] for g in range(G)]
        per_seq.append(jnp.concatenate(heads, axis=1))
    y = per_seq[0] if nb == 1 else jnp.concatenate(per_seq, axis=0)
    y_ref[...] = y.astype(y_ref.dtype)


def _swa(z, k_past, v_past, slope_col, sink_col, ybuf, row0, B, T, Tq, KV, G, dh, off_q, off_k, off_v):
    nq = T // Tq
    nb = 1 if Tq % BF16_SUBLANES == 0 else BF16_SUBLANES // Tq
    assert nb == 1 or nq == 1
    R = nb * Tq
    assert B % nb == 0 and row0 % R == 0
    rb0 = row0 // R
    QW, KW = KV * G * dh, KV * dh
    assert off_q % QW == 0 and off_k % KW == 0 and off_v % KW == 0

    def own(width, off):
        return pl.BlockSpec((R, width), lambda b, n: (rb0 + b * nq + n, off // width))

    if k_past is None:
        assert Tq == WINDOW and nb == 1

        def past(off):
            return pl.BlockSpec((WINDOW, KW), lambda b, n: (jnp.maximum(rb0 + b * nq + n - 1, 0), off // KW))

        past_specs, past_args = [past(off_k), past(off_v)], [z, z]
    else:
        spec = pl.BlockSpec((nb, WINDOW, KW), lambda b, n: (b, 0, 0))
        past_specs, past_args = [spec, spec], [k_past, v_past]

    const = pl.BlockSpec((KV, G * Tq, 1), lambda b, n: (0, 0, 0))
    return pl.pallas_call(
        functools.partial(_swa_kernel, Tq=Tq, nb=nb, KV=KV, G=G, dh=dh,
                          first_block_has_no_past=k_past is None),
        grid=(B // nb, nq),
        in_specs=[own(QW, off_q), own(KW, off_k), own(KW, off_v)] + past_specs
                 + [const, const, pl.BlockSpec(memory_space=pl.ANY)],
        out_specs=pl.BlockSpec((R, QW), lambda b, n: (rb0 + b * nq + n, 0)),
        out_shape=jax.ShapeDtypeStruct(ybuf.shape, ybuf.dtype),
        input_output_aliases={7: 0},
        compiler_params=_params("parallel", "arbitrary"),
        name="swa",
    )(z, z, z, *past_args, slope_col, sink_col, ybuf)


def _pool_kernel(u_ref, halo_ref, w_ref, sc_ref, ybuf_ref, y_ref,
                 *, Tt, nb, P, G, Cg, first_block_has_no_past):
    del ybuf_ref
    n = pl.program_id(1)
    t_glob = n * Tt + lax.broadcasted_iota(jnp.int32, (Tt, 1), 0)
    per_seq = []
    for j in range(nb):
        u = u_ref[j * Tt:(j + 1) * Tt, :]
        halo = halo_ref[j] if len(halo_ref.shape) == 3 else halo_ref[...]
        if first_block_has_no_past:
            halo = jnp.where(n > 0, halo, 0.0)
        ext = jnp.concatenate([halo, u], axis=0)
        groups = []
        for g, w in enumerate(POOL_WINDOWS):
            cs = slice(g * Cg, (g + 1) * Cg)
            s = ext[:, cs]
            d = 1
            while d < w:
                s = s + pltpu.roll(s, d, 0)
                d *= 2
            count = jnp.minimum(w, P + 1 + t_glob).astype(F32)
            y = s[POOL_HALO:, :] / count - u[:, cs]
            groups.append(jnp.dot(y.astype(BF16), w_ref[g], preferred_element_type=F32))
        per_seq.append(jnp.concatenate(groups, axis=1) * sc_ref[...])
    y = per_seq[0] if nb == 1 else jnp.concatenate(per_seq, axis=0)
    y_ref[...] = y.astype(y_ref.dtype)


def _pool(z, prefix, w_pool, pool_scale, ybuf, row0, B, T, Tt, off_u):
    G, Cg, _ = w_pool.shape
    PD = G * Cg
    nt = T // Tt
    nb = 1 if Tt % BF16_SUBLANES == 0 else BF16_SUBLANES // Tt
    assert nb == 1 or nt == 1
    R = nb * Tt
    assert B % nb == 0 and row0 % R == 0 and off_u % PD == 0
    assert Tt % POOL_HALO == 0 or prefix is not None
    rb0 = row0 // R
    if prefix is None:
        hb = Tt // POOL_HALO
        halo_spec = pl.BlockSpec((POOL_HALO, PD),
                                 lambda b, n: (jnp.maximum((rb0 + b * nt + n) * hb - 1, 0), off_u // PD))
        halo_arg, P = z, 0
    else:
        halo_spec = pl.BlockSpec((nb, POOL_HALO, PD), lambda b, n: (b, 0, 0))
        halo_arg, P = prefix, POOL_HALO - 1
    return pl.pallas_call(
        functools.partial(_pool_kernel, Tt=Tt, nb=nb, P=P, G=G, Cg=Cg,
                          first_block_has_no_past=prefix is None),
        grid=(B // nb, nt),
        in_specs=[pl.BlockSpec((R, PD), lambda b, n: (rb0 + b * nt + n, off_u // PD)),
                  halo_spec,
                  pl.BlockSpec((G, Cg, Cg), lambda b, n: (0, 0, 0)),
                  pl.BlockSpec((1, PD), lambda b, n: (0, 0)),
                  pl.BlockSpec(memory_space=pl.ANY)],
        out_specs=pl.BlockSpec((R, PD), lambda b, n: (rb0 + b * nt + n, 0)),
        out_shape=jax.ShapeDtypeStruct(ybuf.shape, ybuf.dtype),
        input_output_aliases={4: 0},
        compiler_params=_params("parallel", "arbitrary"),
        name="pool",
    )(z, halo_arg, w_pool, pool_scale, ybuf)


def _plain(dot, extras):
    return dot(0)


def _gate_combine(dot, extras):
    acc = jax.nn.sigmoid(dot(0)) * dot(1)
    acc = acc + jax.nn.sigmoid(dot(2)) * dot(3)
    return acc + jax.nn.sigmoid(dot(4)) * dot(5)


def _residual(dot, extras):
    return extras[0][...] + dot(0)


def _ple(dot, extras):
    return extras[0][...] + dot(1) * jax.nn.sigmoid(dot(0))


def kernel(x_prompt, x_sample, state_gla, cache_swa_k, cache_swa_v, state_pool, state_ffn_conv, p_prompt, p_sample, g_mix, w_in, w_gla_a2, b_gla_a, g_gla_out, swa_sinks, w_pool, pool_scale, w_branch_gla, w_branch_swa, w_branch_pool, w_gate, w_out, g_ffn, w_up, w_conv, b_conv, w_down, g_ple, w_ple, w_ple_gate, g_final):
    Bp, Tp, D = x_prompt.shape
    Bs, Ts, _ = x_sample.shape
    depth = w_in.shape[0]
    _, _, H, DK, DV = state_gla.shape
    rank = w_gla_a2.shape[1]
    _, _, WB, KV, dh = cache_swa_k.shape
    NH = swa_sinks.shape[1]
    G = NH // KV
    _, PG, Cg, _ = w_pool.shape
    PD = PG * Cg
    P = state_pool.shape[2]
    F = w_down.shape[1]
    CW = w_conv.shape[1]
    assert WB == WINDOW and P == POOL_HALO - 1 and CW == 3 and PG == len(POOL_WINDOWS)
    assert Tp >= WINDOW and Tp >= P and Ts >= CW - 1
    GK, GV, SQ, SKV = H * DK, H * DV, NH * dh, KV * dh
    Mp, Ms = Bp * Tp, Bs * Ts
    M = Mp + Ms

    off_q, off_k, off_v, off_r = 0, GK, 2 * GK, 2 * GK + GV
    off_u = 2 * GK + 2 * GV
    off_sq = off_u + PD
    off_sk = off_sq + SQ
    off_sv = off_sk + SKV
    NZ = off_sv + SKV
    in_sizes = (GK, GK, GV, GV, rank, SQ, SKV, SKV, PD)
    in_off = [0]
    for s in in_sizes:
        in_off.append(in_off[-1] + s)
    segments = ((0, in_off[4]), (in_off[8], in_off[9]), (in_off[5], in_off[8]))

    tm = _pick_tile(M, 1056, BF16_SUBLANES)
    tm_half = _pick_tile(M, 528, BF16_SUBLANES)
    tf = _pick_tile(F, 256, LANES)
    tk_down = _pick_tile(F, 5504, LANES)
    tn_z = _pick_tile(NZ, 768, LANES)
    tn_d = _pick_tile(D, 512, LANES)
    tn_g = _pick_tile(D, 256, LANES)
    Tt_pool = _pick_tile(Tp, 256, POOL_HALO)
    M_tail = Bp * SUBLANES + Ms

    w_main_b, w_lr_b = _cast_in_proj(jnp.swapaxes(w_in, 1, 2), segments, (in_off[4], in_off[5]))
    w_br_b = [_cast_weights(w_branch_gla), _cast_weights(w_branch_swa), _cast_weights(w_branch_pool)]
    w_gate_b, w_out_b, w_up_b, w_down_b, w_pg_b = (
        _cast_weights(w, 0) for w in (w_gate, w_out, w_up, w_down, w_ple_gate))
    w_ple_b = w_ple.astype(BF16)
    w_a2_b = w_gla_a2.astype(BF16)
    w_pool_b = w_pool.astype(BF16)
    b_conv3 = b_conv.reshape(depth, 1, F)

    heads = jnp.arange(1, NH + 1, dtype=F32)
    slopes = jnp.exp2(-8.0 * heads / NH).reshape(KV, G, 1)

    qi = jnp.arange(WINDOW, dtype=jnp.int32)[:, None]
    kj = jnp.arange(2 * WINDOW, dtype=jnp.int32)[None, :]
    dist = qi + WINDOW - kj
    in_win = (dist >= 0) & (dist < WINDOW)
    has_key = jnp.stack([in_win & (kj >= WINDOW), in_win])
    swa_bias = jnp.where(has_key[:, None], -(slopes.reshape(NH, 1, 1) * dist.astype(F32))[None], -jnp.inf)

    def head_cols(v, tq):
        return jnp.broadcast_to(v[:, :, None, :], (KV, G, tq, 1)).reshape(KV, G * tq, 1)

    x = jnp.concatenate([x_prompt.reshape(Mp, D), x_sample.reshape(Ms, D)], axis=0)
    p_all = jnp.concatenate([p_prompt.reshape(depth, Mp, -1), p_sample.reshape(depth, Ms, -1)],
                            axis=1).astype(BF16)
    gla_p = jnp.zeros((depth, Bp, H, DK, DV), F32)
    gla_s = jnp.zeros((depth, Bs, H, DK, DV), F32)
    ya = jnp.zeros((M, GV), BF16)
    yb = jnp.zeros((M, SQ), BF16)
    yc = jnp.zeros((M, PD), BF16)
    tail_rows = [slice((b + 1) * Tp - SUBLANES, (b + 1) * Tp) for b in range(Bp)]
    outs_p, outs_s = [], []

    for i in range(depth):
        b_a = b_gla_a[i].reshape(1, GK)
        g_o = g_gla_out[i].reshape(1, GV)
        p_scale = pool_scale[i].reshape(1, PD)
        sinks = swa_sinks[i].reshape(KV, G, 1)

        h = _rmsnorm(x, g_mix[i], BF16)
        nxt = i + 1 if i + 1 < depth else None

        def side(w):
            return None if nxt is None else (w, nxt)

        z = _fused_matmul([h], [(0, w_main_b, i, 0)], [], _plain, NZ, F32, tm, tn_z, "in_proj")
        zlr = _fused_matmul([h], [(0, w_lr_b, i, 0)], [], _plain, LANES, F32, tm, LANES, "in_proj_lr")

        gla_args = (w_a2_b[i], b_a, g_o)
        gla_dims = (H, DK, DV, off_q, off_k, off_v, off_r)
        ya, gla_p = _gla(z, zlr, *gla_args, None, ya, gla_p, i, 0, Bp, Tp, *gla_dims)
        ya, gla_s = _gla(z, zlr, *gla_args, state_gla, ya, gla_s, i, Mp, Bs, Ts, *gla_dims)

        swa_dims = (KV, G, dh, off_sq, off_sk, off_sv)
        yb = _swa_blocks(z, swa_bias, swa_sinks[i], yb, Bp, Tp, *swa_dims)
        ck = cache_swa_k[i].reshape(Bs, WB, SKV)
        cv = cache_swa_v[i].reshape(Bs, WB, SKV)
        yb = _swa(z, ck, cv, head_cols(slopes, Ts), head_cols(sinks, Ts), yb, Mp, Bs, Ts, Ts, *swa_dims)

        yc = _pool(z, None, w_pool_b[i], p_scale, yc, 0, Bp, Tp, Tt_pool, off_u)
        pool_prefix = jnp.pad(state_pool[i], ((0, 0), (POOL_HALO - P, 0), (0, 0)))
        yc = _pool(z, pool_prefix, w_pool_b[i], p_scale, yc, Mp, Bs, Ts, Ts, off_u)

        m = _fused_matmul([h, ya, yb, yc],
                          [(0, w_gate_b, None, 0), (1, w_br_b[0], i, 0), (0, w_gate_b, None, D),
                           (2, w_br_b[1], i, 0), (0, w_gate_b, None, 2 * D), (3, w_br_b[2], i, 0)],
                          [], _gate_combine, D, BF16, tm_half, tn_g, "gate_combine", side(w_gate))
        if nxt is not None:
            m, w_gate_n = m
        x = _fused_matmul([m], [(0, w_out_b, None, 0)], [x], _residual, D, F32, tm, tn_d, "out_proj",
                          side(w_out))
        if nxt is not None:
            x, w_out_n = x

        h = _rmsnorm(x, g_ffn[i], BF16)
        pre = state_ffn_conv[i]
        r0 = jnp.broadcast_to(pre[:, 0:1], (Bs, Ts, F)).reshape(Ms, F)
        r1 = jnp.broadcast_to(pre[:, 1:2], (Bs, Ts, F)).reshape(Ms, F)
        act = _up_conv(h, w_up_b, w_conv, b_conv3, r0, r1, i, Mp, Tp, Ts, tm, tf, side(w_up))
        if nxt is not None:
            act, w_up_n = act
        x = _residual_matmul_ksplit(act, w_down_b, x, tm, tn_d, tk_down, "down_proj", side(w_down))
        if nxt is not None:
            x, w_down_n = x
        h_tail = jnp.concatenate([h[r] for r in tail_rows] + [h[Mp:]], axis=0)
        a_tail = _fused_matmul([h_tail], [(0, w_up_b, None, 0)], [], _plain, F, F32, M_tail, tf, "up_proj_tail")

        h = _rmsnorm(x, g_ple[i], BF16)
        x = _fused_matmul([h, p_all[i]], [(0, w_pg_b, None, 0), (1, w_ple_b, i, 0)], [x], _ple, D, F32,
                          tm, tn_d, "ple", side(w_ple_gate))
        if nxt is not None:
            x, w_pg_n = x
            w_gate_b, w_out_b, w_up_b, w_down_b, w_pg_b = w_gate_n, w_out_n, w_up_n, w_down_n, w_pg_n

        wb = min(WINDOW, Tp)

        def prompt_tail(n_rows, off, width):
            return jnp.stack([z[(b + 1) * Tp - n_rows:(b + 1) * Tp, off:off + width] for b in range(Bp)])

        zs = z[Mp:].reshape(Bs, Ts, NZ)
        swa_k_p = prompt_tail(wb, off_sk, SKV).reshape(Bp, wb, KV, dh)
        swa_v_p = prompt_tail(wb, off_sv, SKV).reshape(Bp, wb, KV, dh)
        pool_p = prompt_tail(P, off_u, PD)
        conv_p = a_tail[:Bp * SUBLANES].reshape(Bp, SUBLANES, F)[:, SUBLANES - (CW - 1):]
        swa_k_s = jnp.concatenate([ck, zs[:, :, off_sk:off_sk + SKV]], axis=1)[:, Ts:].reshape(Bs, WB, KV, dh)
        swa_v_s = jnp.concatenate([cv, zs[:, :, off_sv:off_sv + SKV]], axis=1)[:, Ts:].reshape(Bs, WB, KV, dh)
        pool_s = jnp.concatenate([state_pool[i], zs[:, :, off_u:off_u + PD]], axis=1)[:, Ts:]
        conv_s = a_tail[Bp * SUBLANES:].reshape(Bs, Ts, F)[:, Ts - (CW - 1):]
        outs_p.append((swa_k_p, swa_v_p, pool_p, conv_p))
        outs_s.append((swa_k_s, swa_v_s, pool_s, conv_s))

    y_prompt = _rmsnorm(x, g_final, F32, 0, Mp).reshape(Bp, Tp, D)
    y_sample = _rmsnorm(x, g_final, F32, Mp, Ms).reshape(Bs, Ts, D)
    stack = lambda outs, c: jnp.stack([o[c] for o in outs])
    return (y_prompt, y_sample,
            gla_p, stack(outs_p, 0), stack(outs_p, 1), stack(outs_p, 2), stack(outs_p, 3),
            gla_s, stack(outs_s, 0), stack(outs_s, 1), stack(outs_s, 2), stack(outs_s, 3))
```

```python
import functools
import math

import jax
import jax.numpy as jnp
from jax import lax
from jax.experimental import pallas as pl
from jax.experimental.pallas import tpu as pltpu

EPS = 1e-6
GLA_TAU = 16.0
GLA_CHUNK = 64
WINDOW = 128
POOL_WINDOWS = (2, 4, 8, 16)

VMEM_LIMIT_BYTES = 56 * 1024 * 1024
LANES = 128
SUBLANES = 8
BF16_SUBLANES = 16
POOL_HALO = 16
CAST_BLOCK_BYTES = 6 * 1024 * 1024

F32 = jnp.float32
BF16 = jnp.bfloat16


def _pick_tile(n, target, align):
    best = None
    for t in range(align, min(n, target) + 1, align):
        if n % t == 0:
            best = t
    return n if best is None else best


def _params(*sem):
    return pltpu.CompilerParams(dimension_semantics=sem, vmem_limit_bytes=VMEM_LIMIT_BYTES)


def _cast_kernel(w_ref, o_ref):
    o_ref[...] = w_ref[...].astype(o_ref.dtype)


def _cast_weights(w, layer=None):
    L, K, N = w.shape
    tk = _pick_tile(K, max(BF16_SUBLANES, CAST_BLOCK_BYTES // (4 * N)), BF16_SUBLANES)
    if layer is None:
        spec = pl.BlockSpec((None, tk, N), lambda l, k: (l, k, 0))
        return pl.pallas_call(
            _cast_kernel, grid=(L, K // tk), in_specs=[spec], out_specs=spec,
            out_shape=jax.ShapeDtypeStruct((L, K, N), BF16),
            compiler_params=_params("parallel", "parallel"), name="cast_weights",
        )(w)
    return pl.pallas_call(
        _cast_kernel, grid=(K // tk,),
        in_specs=[pl.BlockSpec((None, tk, N), lambda k: (layer, k, 0))],
        out_specs=pl.BlockSpec((tk, N), lambda k: (k, 0)),
        out_shape=jax.ShapeDtypeStruct((K, N), BF16),
        compiler_params=_params("parallel"), name="cast_layer",
    )(w)


def _side_cast(src, layer, steps, step_of):
    _, K, N = src.shape
    rows = next(r for r in range(BF16_SUBLANES, K + 1, BF16_SUBLANES) if K % r == 0 and K // r <= steps)
    last = K // rows - 1
    in_spec = pl.BlockSpec((None, rows, N), lambda *g: (layer, jnp.minimum(step_of(*g), last), 0))
    out_spec = pl.BlockSpec((rows, N), lambda *g: (jnp.minimum(step_of(*g), last), 0))
    return in_spec, out_spec, jax.ShapeDtypeStruct((K, N), BF16)


def _weight_spec(w, layer, rows, cols, index):
    if w.ndim == 2:
        return pl.BlockSpec((rows, cols), index)
    return pl.BlockSpec((None, rows, cols), lambda *g: (layer,) + tuple(index(*g)))


def _cast_in_proj_kernel(wt_ref, main_ref, lr_ref, *, segments, lr):
    wt = wt_ref[...]
    main = jnp.concatenate([wt[a:b, :] for a, b in segments], axis=0)
    main_ref[...] = main.T.astype(main_ref.dtype)
    pad = jnp.zeros((lr_ref.shape[1] - (lr[1] - lr[0]), wt.shape[1]), F32)
    lr_ref[...] = jnp.concatenate([wt[lr[0]:lr[1], :], pad], axis=0).T.astype(lr_ref.dtype)


def _cast_in_proj(wt, segments, lr):
    L, N, K = wt.shape
    n_main = sum(b - a for a, b in segments)
    tk = _pick_tile(K, LANES, LANES)
    return pl.pallas_call(
        functools.partial(_cast_in_proj_kernel, segments=segments, lr=lr),
        grid=(L, K // tk),
        in_specs=[pl.BlockSpec((None, N, tk), lambda l, k: (l, 0, k))],
        out_specs=[pl.BlockSpec((None, tk, n_main), lambda l, k: (l, k, 0)),
                   pl.BlockSpec((None, tk, LANES), lambda l, k: (l, k, 0))],
        out_shape=[jax.ShapeDtypeStruct((L, K, n_main), BF16),
                   jax.ShapeDtypeStruct((L, K, LANES), BF16)],
        compiler_params=_params("parallel", "parallel"), name="cast_in_proj",
    )(wt)


def _rmsnorm_kernel(x_ref, g_ref, o_ref):
    x = x_ref[...]
    y = x * lax.rsqrt(jnp.mean(x * x, axis=-1, keepdims=True) + EPS)
    o_ref[...] = (y * g_ref[...]).astype(o_ref.dtype)


def _rmsnorm(x, g, out_dtype, row0=0, rows=None):
    M, D = x.shape
    rows = M if rows is None else rows
    tm = _pick_tile(math.gcd(rows, row0) if row0 else rows, 512, BF16_SUBLANES)
    rb0 = row0 // tm
    return pl.pallas_call(
        _rmsnorm_kernel,
        grid=(rows // tm,),
        in_specs=[pl.BlockSpec((tm, D), lambda i: (rb0 + i, 0)),
                  pl.BlockSpec((1, D), lambda i: (0, 0))],
        out_specs=pl.BlockSpec((tm, D), lambda i: (i, 0)),
        out_shape=jax.ShapeDtypeStruct((rows, D), out_dtype),
        compiler_params=_params("parallel"),
        name="rmsnorm",
    )(x, g.reshape(1, D))


def _norm_prep_kernel(x_ref, g_ref, xg_ref, ssq_ref):
    x = x_ref[...]
    xg_ref[...] = (x * g_ref[...]).astype(xg_ref.dtype)
    ssq_ref[...] = jnp.sum(x * x, axis=-1, keepdims=True)


def _norm_prep(x, g):
    M, D = x.shape
    tm = _pick_tile(M, 512, BF16_SUBLANES)
    return pl.pallas_call(
        _norm_prep_kernel,
        grid=(M // tm,),
        in_specs=[pl.BlockSpec((tm, D), lambda i: (i, 0)), pl.BlockSpec((1, D), lambda i: (0, 0))],
        out_specs=[pl.BlockSpec((tm, D), lambda i: (i, 0)), pl.BlockSpec((tm, 1), lambda i: (i, 0))],
        out_shape=[jax.ShapeDtypeStruct((M, D), BF16), jax.ShapeDtypeStruct((M, 1), F32)],
        compiler_params=_params("parallel"),
        name="norm_prep",
    )(x, g.reshape(1, D))


def _row_scale(ssq_ref, d_model):
    return lax.rsqrt(ssq_ref[...] / d_model + EPS)


def _emit_norm_parts(val, g_ref, xg_ref, ssq_ref, j):
    xg_ref[...] = (val * g_ref[...]).astype(xg_ref.dtype)
    part = jnp.sum(val * val, axis=-1, keepdims=True)

    @pl.when(j == 0)
    def _():
        ssq_ref[...] = part

    @pl.when(j > 0)
    def _():
        ssq_ref[...] += part


def _mm_kernel(*refs, n_acts, pair_act, n_extra, combine, scaled, d_model, norm_out, has_cast):
    it = iter(refs)
    act_refs = [next(it) for _ in range(n_acts)]
    w_refs = [next(it) for _ in pair_act]
    extras = [next(it) for _ in range(n_extra)]
    ssq_in = next(it) if scaled else None
    g_ref = next(it) if norm_out else None
    cast_src = next(it) if has_cast else None
    o_ref = next(it)
    xg_ref, ssq_ref = (next(it), next(it)) if norm_out else (None, None)
    if has_cast:
        next(it)[...] = cast_src[...].astype(BF16)
    r = _row_scale(ssq_in, d_model) if scaled else None

    def dot(p):
        d = jnp.dot(act_refs[pair_act[p]][...], w_refs[p][...], preferred_element_type=F32)
        return d * r if p in scaled else d

    val = combine(dot, extras)
    o_ref[...] = val.astype(o_ref.dtype)
    if norm_out:
        _emit_norm_parts(val, g_ref, xg_ref, ssq_ref, pl.program_id(1))


def _fused_matmul(acts, pairs, extras, combine, n_out, out_dtype, tm, tn, name,
                  scale=None, norm_g=None, cast=None):
    M = acts[0].shape[0]
    n_j = n_out // tn
    in_specs = [pl.BlockSpec((tm, a.shape[1]), lambda i, j: (i, 0)) for a in acts]
    args = list(acts)
    for ai, w, layer, off in pairs:
        K = acts[ai].shape[1]
        assert off % tn == 0 and w.shape[-2] == K
        in_specs.append(_weight_spec(w, layer, K, tn, lambda i, j, ob=off // tn: (0, j + ob)))
        args.append(w)
    for e in extras:
        in_specs.append(pl.BlockSpec((tm, tn), lambda i, j: (i, j)))
        args.append(e)
    out_specs = [pl.BlockSpec((tm, tn), lambda i, j: (i, j))]
    out_shape = [jax.ShapeDtypeStruct((M, n_out), out_dtype)]
    scaled, d_model = (), None
    if scale is not None:
        scaled = tuple(scale[1])
        d_model = acts[pairs[scaled[0]][0]].shape[1]
        in_specs.append(pl.BlockSpec((tm, 1), lambda i, j: (i, 0)))
        args.append(scale[0])
    if norm_g is not None:
        in_specs.append(pl.BlockSpec((1, tn), lambda i, j: (0, j)))
        args.append(norm_g.reshape(1, n_out))
        out_specs += [pl.BlockSpec((tm, tn), lambda i, j: (i, j)), pl.BlockSpec((tm, 1), lambda i, j: (i, 0))]
        out_shape += [jax.ShapeDtypeStruct((M, n_out), BF16), jax.ShapeDtypeStruct((M, 1), F32)]
    if cast is not None:
        c_in, c_out, c_shape = _side_cast(cast[0], cast[1], (M // tm) * n_j, lambda i, j: i * n_j + j)
        in_specs.append(c_in)
        args.append(cast[0])
        out_specs.append(c_out)
        out_shape.append(c_shape)
    res = pl.pallas_call(
        functools.partial(_mm_kernel, n_acts=len(acts), pair_act=tuple(p[0] for p in pairs),
                          n_extra=len(extras), combine=combine, scaled=scaled, d_model=d_model,
                          norm_out=norm_g is not None, has_cast=cast is not None),
        grid=(M // tm, n_j),
        in_specs=in_specs,
        out_specs=out_specs,
        out_shape=out_shape,
        compiler_params=_params("arbitrary", "arbitrary"),
        name=name,
    )(*args)
    return res[0] if len(res) == 1 else tuple(res)


def _mm_ksplit_kernel(*refs, nk, has_cast):
    a_ref, w_ref, x_ref, g_ref = refs[:4]
    o_ref, xg_ref, ssq_ref = refs[4 + has_cast:7 + has_cast]
    acc_ref = refs[-1]
    if has_cast:
        refs[8][...] = refs[4][...].astype(BF16)
    k = pl.program_id(2)
    d = jnp.dot(a_ref[...], w_ref[...], preferred_element_type=F32)

    @pl.when(k == 0)
    def _():
        acc_ref[...] = d

    @pl.when(k > 0)
    def _():
        acc_ref[...] += d

    @pl.when(k == nk - 1)
    def _():
        val = x_ref[...] + acc_ref[...]
        o_ref[...] = val
        _emit_norm_parts(val, g_ref, xg_ref, ssq_ref, pl.program_id(1))


def _residual_matmul_ksplit(a, w, x, norm_g, tm, tn, tk, name, cast=None):
    M, K = a.shape
    N = w.shape[1]
    nk, n_j = K // tk, N // tn
    in_specs = [pl.BlockSpec((tm, tk), lambda i, j, k: (i, k)),
                pl.BlockSpec((tk, tn), lambda i, j, k: (k, j)),
                pl.BlockSpec((tm, tn), lambda i, j, k: (i, j)),
                pl.BlockSpec((1, tn), lambda i, j, k: (0, j))]
    out_specs = [pl.BlockSpec((tm, tn), lambda i, j, k: (i, j)),
                 pl.BlockSpec((tm, tn), lambda i, j, k: (i, j)),
                 pl.BlockSpec((tm, 1), lambda i, j, k: (i, 0))]
    out_shape = [jax.ShapeDtypeStruct((M, N), F32), jax.ShapeDtypeStruct((M, N), BF16),
                 jax.ShapeDtypeStruct((M, 1), F32)]
    args = [a, w, x, norm_g.reshape(1, N)]
    if cast is not None:
        c_in, c_out, c_shape = _side_cast(cast[0], cast[1], (M // tm) * n_j * nk,
                                          lambda i, j, k: (i * n_j + j) * nk + k)
        in_specs.append(c_in)
        args.append(cast[0])
        out_specs.append(c_out)
        out_shape.append(c_shape)
    res = pl.pallas_call(
        functools.partial(_mm_ksplit_kernel, nk=nk, has_cast=cast is not None),
        grid=(M // tm, n_j, nk),
        in_specs=in_specs,
        out_specs=out_specs,
        out_shape=out_shape,
        scratch_shapes=[pltpu.VMEM((tm, tn), F32)],
        compiler_params=_params("arbitrary", "arbitrary", "arbitrary"),
        name=name,
    )(*args)
    return tuple(res)


GELU_K0 = -2.0 * math.sqrt(2.0 / math.pi) * math.log2(math.e)
GELU_K1 = 0.044715 * GELU_K0


def _conv_gelu(a, a1, a2, wc_ref, bc_ref):
    x = bc_ref[...] + a2 * wc_ref[0:1, :]
    x = x + a1 * wc_ref[1:2, :]
    x = x + a * wc_ref[2:3, :]
    return x / (1.0 + jnp.exp2(x * (GELU_K0 + GELU_K1 * (x * x))))


def _up_conv_kernel(*refs, tm, Mp, Tp, Ts, Ms, piece, has_cast):
    h_ref, ssq_ref, wa_ref, wg_ref, wc_ref, bc_ref, r0_ref, r1_ref = refs[:8]
    o_ref = refs[8 + has_cast]
    a_scr, g_scr, halo_scr = refs[-3:]
    if has_cast:
        refs[10][...] = refs[8][...].astype(BF16)
    i, j = pl.program_id(0), pl.program_id(1)

    @pl.when(i == 0)
    def _():
        halo_scr[j] = jnp.zeros(halo_scr.shape[1:], F32)

    h = h_ref[...]
    r = _row_scale(ssq_ref, h.shape[1])
    a_scr[0:SUBLANES, :] = halo_scr[j]
    a_scr[SUBLANES:, :] = jnp.dot(h, wa_ref[...], preferred_element_type=F32) * r
    g_scr[...] = jnp.dot(h, wg_ref[...], preferred_element_type=F32) * r
    halo_scr[j] = a_scr[tm:, :]

    row0 = i * tm
    sub = lax.broadcasted_iota(jnp.int32, (SUBLANES, 1), 0)
    for lo in range(0, tm, piece):
        ext = a_scr[lo:lo + SUBLANES + piece, :]
        a = ext[SUBLANES:, :]
        a1 = pltpu.roll(ext, 1, 0)[SUBLANES:, :]
        a2 = pltpu.roll(ext, 2, 0)[SUBLANES:, :]
        if lo < tm - Ms:
            starts = (row0 + lo) % Tp == 0
            top1 = jnp.where(starts & (sub == 0), 0.0, a1[:SUBLANES, :])
            top2 = jnp.where(starts & (sub < 2), 0.0, a2[:SUBLANES, :])
            a1 = jnp.concatenate([top1, a1[SUBLANES:, :]], axis=0)
            a2 = jnp.concatenate([top2, a2[SUBLANES:, :]], axis=0)
        else:
            row = row0 + lo + lax.broadcasted_iota(jnp.int32, (piece, 1), 0)
            is_sample = row >= Mp
            pos = jnp.where(is_sample, (row - Mp) % Ts, row % Tp)
            ls = lo - (tm - Ms)
            hist2 = jnp.where(is_sample, r0_ref[ls:ls + piece, :], 0.0)
            hist1 = jnp.where(is_sample, r1_ref[ls:ls + piece, :], 0.0)
            a1 = jnp.where(pos == 0, hist1, a1)
            a2 = jnp.where(pos == 0, hist2, jnp.where(pos == 1, hist1, a2))
        act = _conv_gelu(a, a1, a2, wc_ref, bc_ref) * g_scr[lo:lo + piece, :]
        o_ref[lo:lo + piece, :] = act.astype(o_ref.dtype)


def _up_conv(h, ssq, w_up, w_conv, b_conv, r0, r1, layer, Mp, Tp, Ts, tm, tf, cast=None):
    M, K = h.shape
    F = w_conv.shape[2]
    Ms = r0.shape[0]
    n_i, n_f = M // tm, F // tf
    piece = next(p for p in (2 * BF16_SUBLANES, BF16_SUBLANES) if tm % p == 0 and (tm - Ms) % p == 0)
    assert M % tm == 0 and F % tf == 0 and M - Mp == Ms and Ms <= tm
    assert tm % piece == 0 and (tm - Ms) % piece == 0 and Tp % piece == 0
    in_specs = [pl.BlockSpec((tm, K), lambda i, j: (i, 0)),
                pl.BlockSpec((tm, 1), lambda i, j: (i, 0)),
                pl.BlockSpec((K, tf), lambda i, j: (0, j)),
                pl.BlockSpec((K, tf), lambda i, j: (0, n_f + j)),
                pl.BlockSpec((None, w_conv.shape[1], tf), lambda i, j: (layer, 0, j)),
                pl.BlockSpec((None, 1, tf), lambda i, j: (layer, 0, j)),
                pl.BlockSpec((Ms, tf), lambda i, j: (0, j)),
                pl.BlockSpec((Ms, tf), lambda i, j: (0, j))]
    out_specs = [pl.BlockSpec((tm, tf), lambda i, j: (i, j))]
    out_shape = [jax.ShapeDtypeStruct((M, F), BF16)]
    args = [h, ssq, w_up, w_up, w_conv, b_conv, r0, r1]
    if cast is not None:
        c_in, c_out, c_shape = _side_cast(cast[0], cast[1], n_i * n_f, lambda i, j: i * n_f + j)
        in_specs.append(c_in)
        args.append(cast[0])
        out_specs.append(c_out)
        out_shape.append(c_shape)
    res = pl.pallas_call(
        functools.partial(_up_conv_kernel, tm=tm, Mp=Mp, Tp=Tp, Ts=Ts, Ms=Ms, piece=piece,
                          has_cast=cast is not None),
        grid=(n_i, n_f),
        in_specs=in_specs,
        out_specs=out_specs,
        out_shape=out_shape,
        scratch_shapes=[pltpu.VMEM((SUBLANES + tm, tf), F32), pltpu.VMEM((tm, tf), F32),
                        pltpu.VMEM((n_f, SUBLANES, tf), F32)],
        compiler_params=_params("arbitrary", "arbitrary"),
        name="up_conv",
    )(*args)
    return res[0] if cast is None else res


def _gla_chunk(q, k, v, glr, w_a2, b_a, S, C):
    DK = q.shape[1]
    la = jnp.dot(glr.astype(BF16), w_a2, preferred_element_type=F32) + b_a
    log_a = (jnp.minimum(la, 0.0) - jnp.log1p(jnp.exp(-jnp.abs(la)))) / GLA_TAU
    row = lax.broadcasted_iota(jnp.int32, (C, DK), 0)
    b = log_a
    s = 1
    while s < C:
        b = b + jnp.where(row >= s, pltpu.roll(b, s, 0), 0.0)
        s *= 2
    b_last = b[C - 1:C, :]
    q_t = ((q * (DK ** -0.5)) * jnp.exp(b)).astype(BF16)
    k_t = (k * jnp.exp(-b)).astype(BF16)
    att = lax.dot_general(q_t, k_t, (((1,), (1,)), ((), ())), preferred_element_type=F32)
    ri = lax.broadcasted_iota(jnp.int32, (C, C), 0)
    ci = lax.broadcasted_iota(jnp.int32, (C, C), 1)
    att = jnp.where(ri >= ci, att, 0.0)
    vb = v.astype(BF16)
    o = (jnp.dot(att.astype(BF16), vb, preferred_element_type=F32)
         + jnp.dot(q_t, S.astype(BF16), preferred_element_type=F32))
    k_end = (k * jnp.exp(b_last - b)).astype(BF16)
    er = lax.broadcasted_iota(jnp.int32, (DK, DK), 0)
    ec = lax.broadcasted_iota(jnp.int32, (DK, DK), 1)
    decay_col = jnp.sum(jnp.where(er == ec, jnp.exp(b_last), 0.0), axis=1, keepdims=True)
    S_new = decay_col * S + lax.dot_general(k_end, vb, (((0,), (0,)), ((), ())),
                                            preferred_element_type=F32)
    return o, S_new


def _gla_kernel(*refs, C, nb, H, DK, DV, rank, n_chunks, has_s0):
    q_ref, k_ref, v_ref, r_ref, lr_ref, wa_ref, ba_ref, go_ref = refs[:8]
    s0_ref = refs[8] if has_s0 else None
    y_ref, sout_ref, s_scr = refs[-3:]
    n = pl.program_id(1)

    @pl.when(n == 0)
    def _():
        s_scr[...] = s0_ref[...] if has_s0 else jnp.zeros(s_scr.shape, F32)

    per_seq = []
    for j in range(nb):
        rows = slice(j * C, (j + 1) * C)
        glr = lr_ref[rows, :rank]
        heads = []
        for h in range(H):
            kc, vc = slice(h * DK, (h + 1) * DK), slice(h * DV, (h + 1) * DV)
            o, s_new = _gla_chunk(q_ref[rows, kc], k_ref[rows, kc], v_ref[rows, vc], glr,
                                  wa_ref[:, kc], ba_ref[:, kc], s_scr[j, h], C)
            s_scr[j, h] = s_new
            o = o * lax.rsqrt(jnp.mean(o * o, axis=-1, keepdims=True) + EPS) * go_ref[:, vc]
            r = r_ref[rows, vc]
            heads.append(o * (r * jax.nn.sigmoid(r)))
        per_seq.append(jnp.concatenate(heads, axis=1))
    y = per_seq[0] if nb == 1 else jnp.concatenate(per_seq, axis=0)
    y_ref[...] = y.astype(y_ref.dtype)

    @pl.when(n == n_chunks - 1)
    def _():
        sout_ref[...] = s_scr[...]


def _gla(z, zlr, w_a2, b_a, g_out, s0, ybuf, sbuf, layer, row0, B, T, H, DK, DV, off_q, off_k, off_v, off_r):
    rank = w_a2.shape[0]
    GK, GV = H * DK, H * DV
    C = math.gcd(T, GLA_CHUNK)
    n_chunks = T // C
    nb = 1 if C % BF16_SUBLANES == 0 else BF16_SUBLANES // C
    assert nb == 1 or n_chunks == 1
    assert B % nb == 0 and row0 % (nb * C) == 0
    assert off_q % GK == 0 and off_k % GK == 0 and off_v % GV == 0 and off_r % GV == 0
    R = nb * C
    rb0 = row0 // R

    def zspec(width, off):
        return pl.BlockSpec((R, width), lambda b, n: (rb0 + b * n_chunks + n, off // width))

    state_spec = pl.BlockSpec((None, nb, H, DK, DV), lambda b, n: (layer, b, 0, 0, 0))
    full = lambda a: pl.BlockSpec(a.shape, lambda b, n: (0,) * a.ndim)
    any_spec = pl.BlockSpec(memory_space=pl.ANY)
    in_specs = [zspec(GK, off_q), zspec(GK, off_k), zspec(GV, off_v), zspec(GV, off_r),
                pl.BlockSpec((R, zlr.shape[1]), lambda b, n: (rb0 + b * n_chunks + n, 0)),
                full(w_a2), full(b_a), full(g_out)]
    args = [z, z, z, z, zlr, w_a2, b_a, g_out]
    if s0 is not None:
        in_specs.append(state_spec)
        args.append(s0)
    n_in = len(args)
    return pl.pallas_call(
        functools.partial(_gla_kernel, C=C, nb=nb, H=H, DK=DK, DV=DV, rank=rank, n_chunks=n_chunks,
                          has_s0=s0 is not None),
        grid=(B // nb, n_chunks),
        in_specs=in_specs + [any_spec, any_spec],
        out_specs=[pl.BlockSpec((R, GV), lambda b, n: (rb0 + b * n_chunks + n, 0)), state_spec],
        out_shape=[jax.ShapeDtypeStruct(ybuf.shape, ybuf.dtype),
                   jax.ShapeDtypeStruct(sbuf.shape, sbuf.dtype)],
        scratch_shapes=[pltpu.VMEM((nb, H, DK, DV), F32)],
        input_output_aliases={n_in: 0, n_in + 1: 1},
        compiler_params=_params("parallel", "arbitrary"),
        name="gla",
    )(*args, ybuf, sbuf)


def _swa_block_kernel(q_ref, k_ref, v_ref, kp_ref, vp_ref, bias_ref, sink_ref, ybuf_ref, y_ref,
                      *, KV, G, dh):
    del ybuf_ref
    scale = dh ** -0.5
    fold_scale = math.frexp(scale)[0] == 0.5
    heads = []
    for kv in range(KV):
        cs = slice(kv * dh, (kv + 1) * dh)
        kk = jnp.concatenate([kp_ref[:, cs], k_ref[:, cs]], axis=0).astype(BF16)
        vv = jnp.concatenate([vp_ref[:, cs], v_ref[:, cs]], axis=0).astype(BF16)
        for g in range(G):
            h = kv * G + g
            q = q_ref[:, h * dh:(h + 1) * dh]
            q = (q * scale if fold_scale else q).astype(BF16)
            s = lax.dot_general(q, kk, (((1,), (1,)), ((), ())), preferred_element_type=F32)
            s = (s if fold_scale else s * scale) + bias_ref[h]
            sink = sink_ref[h]
            m = jnp.maximum(jnp.max(s, axis=1, keepdims=True), sink)
            e = jnp.exp(s - m)
            denom = jnp.sum(e, axis=1, keepdims=True) + jnp.exp(sink - m)
            heads.append(jnp.dot((e / denom).astype(BF16), vv, preferred_element_type=F32))
    y_ref[...] = jnp.concatenate(heads, axis=1).astype(y_ref.dtype)


def _swa_blocks(z, bias, sinks, ybuf, B, T, KV, G, dh, off_q, off_k, off_v):
    nq = T // WINDOW
    QW, KW = KV * G * dh, KV * dh
    assert T % WINDOW == 0 and off_q % QW == 0 and off_k % KW == 0 and off_v % KW == 0

    def own(width, off):
        return pl.BlockSpec((WINDOW, width), lambda b, n: (b * nq + n, off // width))

    def past(off):
        return pl.BlockSpec((WINDOW, KW), lambda b, n: (jnp.maximum(b * nq + n - 1, 0), off // KW))

    return pl.pallas_call(
        functools.partial(_swa_block_kernel, KV=KV, G=G, dh=dh),
        grid=(B, nq),
        in_specs=[own(QW, off_q), own(KW, off_k), own(KW, off_v), past(off_k), past(off_v),
                  pl.BlockSpec((None,) + bias.shape[1:], lambda b, n: (jnp.minimum(n, 1), 0, 0, 0)),
                  pl.BlockSpec(memory_space=pltpu.SMEM),
                  pl.BlockSpec(memory_space=pl.ANY)],
        out_specs=pl.BlockSpec((WINDOW, QW), lambda b, n: (b * nq + n, 0)),
        out_shape=jax.ShapeDtypeStruct(ybuf.shape, ybuf.dtype),
        input_output_aliases={7: 0},
        compiler_params=_params("parallel", "arbitrary"),
        name="swa_blocks",
    )(z, z, z, z, z, bias, sinks, ybuf)


def _swa_kernel(q_ref, k_ref, v_ref, kp_ref, vp_ref, slope_ref, sink_ref, ybuf_ref, y_ref,
                *, Tq, nb, KV, G, dh, first_block_has_no_past):
    del ybuf_ref
    n = pl.program_id(1)
    W = WINDOW
    rows, keys = G * Tq, W + Tq
    ri = lax.broadcasted_iota(jnp.int32, (rows, keys), 0)
    ci = lax.broadcasted_iota(jnp.int32, (rows, keys), 1)
    dist = (ri % Tq) + W - ci
    valid = (dist >= 0) & (dist < WINDOW)
    if first_block_has_no_past:
        valid = valid & ((ci >= W) | (n > 0))
    distf = dist.astype(F32)
    per_seq = []
    for j in range(nb):
        rs = slice(j * Tq, (j + 1) * Tq)
        q, k_own, v_own = q_ref[rs, :], k_ref[rs, :], v_ref[rs, :]
        k_past = kp_ref[j] if len(kp_ref.shape) == 3 else kp_ref[...]
        v_past = vp_ref[j] if len(vp_ref.shape) == 3 else vp_ref[...]
        heads = []
        for kv in range(KV):
            cs = slice(kv * dh, (kv + 1) * dh)
            kk = jnp.concatenate([k_past[:, cs], k_own[:, cs]], axis=0).astype(BF16)
            vv = jnp.concatenate([v_past[:, cs], v_own[:, cs]], axis=0).astype(BF16)
            qs = jnp.concatenate(
                [q[:, (kv * G + g) * dh:(kv * G + g + 1) * dh] for g in range(G)], axis=0).astype(BF16)
            s = lax.dot_general(qs, kk, (((1,), (1,)), ((), ())), preferred_element_type=F32)
            s = s * (dh ** -0.5) - slope_ref[kv] * distf
            s = jnp.where(valid, s, -jnp.inf)
            sink = sink_ref[kv]
            m = jnp.maximum(jnp.max(s, axis=1, keepdims=True), sink)
            e = jnp.exp(s - m)
            denom = jnp.sum(e, axis=1, keepdims=True) + jnp.exp(sink - m)
            o = jnp.dot((e / denom).astype(BF16), vv, preferred_element_type=F32)
            heads += [o[g * Tq:(g + 1) * Tq, :] for g in range(G)]
        per_seq.append(jnp.concatenate(heads, axis=1))
    y = per_seq[0] if nb == 1 else jnp.concatenate(per_seq, axis=0)
    y_ref[...] = y.astype(y_ref.dtype)


def _swa(z, k_past, v_past, slope_col, sink_col, ybuf, row0, B, T, Tq, KV, G, dh, off_q, off_k, off_v):
    nq = T // Tq
    nb = 1 if Tq % BF16_SUBLANES == 0 else BF16_SUBLANES // Tq
    assert nb == 1 or nq == 1
    R = nb * Tq
    assert B % nb == 0 and row0 % R == 0
    rb0 = row0 // R
    QW, KW = KV * G * dh, KV * dh
    assert off_q % QW == 0 and off_k % KW == 0 and off_v % KW == 0

    def own(width, off):
        return pl.BlockSpec((R, width), lambda b, n: (rb0 + b * nq + n, off // width))

    if k_past is None:
        assert Tq == WINDOW and nb == 1

        def past(off):
            return pl.BlockSpec((WINDOW, KW), lambda b, n: (jnp.maximum(rb0 + b * nq + n - 1, 0), off // KW))

        past_specs, past_args = [past(off_k), past(off_v)], [z, z]
    else:
        spec = pl.BlockSpec((nb, WINDOW, KW), lambda b, n: (b, 0, 0))
        past_specs, past_args = [spec, spec], [k_past, v_past]

    const = pl.BlockSpec((KV, G * Tq, 1), lambda b, n: (0, 0, 0))
    return pl.pallas_call(
        functools.partial(_swa_kernel, Tq=Tq, nb=nb, KV=KV, G=G, dh=dh,
                          first_block_has_no_past=k_past is None),
        grid=(B // nb, nq),
        in_specs=[own(QW, off_q), own(KW, off_k), own(KW, off_v)] + past_specs
                 + [const, const, pl.BlockSpec(memory_space=pl.ANY)],
        out_specs=pl.BlockSpec((R, QW), lambda b, n: (rb0 + b * nq + n, 0)),
        out_shape=jax.ShapeDtypeStruct(ybuf.shape, ybuf.dtype),
        input_output_aliases={7: 0},
        compiler_params=_params("parallel", "arbitrary"),
        name="swa",
    )(z, z, z, *past_args, slope_col, sink_col, ybuf)


def _pool_kernel(u_ref, halo_ref, w_ref, sc_ref, ybuf_ref, y_ref,
                 *, Tt, nb, P, G, Cg, first_block_has_no_past):
    del ybuf_ref
    n = pl.program_id(1)
    t_glob = n * Tt + lax.broadcasted_iota(jnp.int32, (Tt, 1), 0)
    per_seq = []
    for j in range(nb):
        u = u_ref[j * Tt:(j + 1) * Tt, :]
        halo = halo_ref[j] if len(halo_ref.shape) == 3 else halo_ref[...]
        if first_block_has_no_past:
            halo = jnp.where(n > 0, halo, 0.0)
        ext = jnp.concatenate([halo, u], axis=0)
        groups = []
        for g, w in enumerate(POOL_WINDOWS):
            cs = slice(g * Cg, (g + 1) * Cg)
            s = ext[:, cs]
            d = 1
            while d < w:
                s = s + pltpu.roll(s, d, 0)
                d *= 2
            count = jnp.minimum(w, P + 1 + t_glob).astype(F32)
            y = s[POOL_HALO:, :] / count - u[:, cs]
            groups.append(jnp.dot(y.astype(BF16), w_ref[g], preferred_element_type=F32))
        per_seq.append(jnp.concatenate(groups, axis=1) * sc_ref[...])
    y = per_seq[0] if nb == 1 else jnp.concatenate(per_seq, axis=0)
    y_ref[...] = y.astype(y_ref.dtype)


def _pool(z, prefix, w_pool, pool_scale, ybuf, row0, B, T, Tt, off_u):
    G, Cg, _ = w_pool.shape
    PD = G * Cg
    nt = T // Tt
    nb = 1 if Tt % BF16_SUBLANES == 0 else BF16_SUBLANES // Tt
    assert nb == 1 or nt == 1
    R = nb * Tt
    assert B % nb == 0 and row0 % R == 0 and off_u % PD == 0
    assert Tt % POOL_HALO == 0 or prefix is not None
    rb0 = row0 // R
    if prefix is None:
        hb = Tt // POOL_HALO
        halo_spec = pl.BlockSpec((POOL_HALO, PD),
                                 lambda b, n: (jnp.maximum((rb0 + b * nt + n) * hb - 1, 0), off_u // PD))
        halo_arg, P = z, 0
    else:
        halo_spec = pl.BlockSpec((nb, POOL_HALO, PD), lambda b, n: (b, 0, 0))
        halo_arg, P = prefix, POOL_HALO - 1
    return pl.pallas_call(
        functools.partial(_pool_kernel, Tt=Tt, nb=nb, P=P, G=G, Cg=Cg,
                          first_block_has_no_past=prefix is None),
        grid=(B // nb, nt),
        in_specs=[pl.BlockSpec((R, PD), lambda b, n: (rb0 + b * nt + n, off_u // PD)),
                  halo_spec,
                  pl.BlockSpec((G, Cg, Cg), lambda b, n: (0, 0, 0)),
                  pl.BlockSpec((1, PD), lambda b, n: (0, 0)),
                  pl.BlockSpec(memory_space=pl.ANY)],
        out_specs=pl.BlockSpec((R, PD), lambda b, n: (rb0 + b * nt + n, 0)),
        out_shape=jax.ShapeDtypeStruct(ybuf.shape, ybuf.dtype),
        input_output_aliases={4: 0},
        compiler_params=_params("parallel", "arbitrary"),
        name="pool",
    )(z, halo_arg, w_pool, pool_scale, ybuf)


def _plain(dot, extras):
    return dot(0)


def _gate_combine(dot, extras):
    acc = jax.nn.sigmoid(dot(0)) * dot(1)
    acc = acc + jax.nn.sigmoid(dot(2)) * dot(3)
    return acc + jax.nn.sigmoid(dot(4)) * dot(5)


def _residual(dot, extras):
    return extras[0][...] + dot(0)


def _ple(dot, extras):
    return extras[0][...] + dot(1) * jax.nn.sigmoid(dot(0))


def kernel(x_prompt, x_sample, state_gla, cache_swa_k, cache_swa_v, state_pool, state_ffn_conv, p_prompt, p_sample, g_mix, w_in, w_gla_a2, b_gla_a, g_gla_out, swa_sinks, w_pool, pool_scale, w_branch_gla, w_branch_swa, w_branch_pool, w_gate, w_out, g_ffn, w_up, w_conv, b_conv, w_down, g_ple, w_ple, w_ple_gate, g_final):
    Bp, Tp, D = x_prompt.shape
    Bs, Ts, _ = x_sample.shape
    depth = w_in.shape[0]
    _, _, H, DK, DV = state_gla.shape
    rank = w_gla_a2.shape[1]
    _, _, WB, KV, dh = cache_swa_k.shape
    NH = swa_sinks.shape[1]
    G = NH // KV
    _, PG, Cg, _ = w_pool.shape
    PD = PG * Cg
    P = state_pool.shape[2]
    F = w_down.shape[1]
    CW = w_conv.shape[1]
    assert WB == WINDOW and P == POOL_HALO - 1 and CW == 3 and PG == len(POOL_WINDOWS)
    assert Tp >= WINDOW and Tp >= P and Ts >= CW - 1
    GK, GV, SQ, SKV = H * DK, H * DV, NH * dh, KV * dh
    Mp, Ms = Bp * Tp, Bs * Ts
    M = Mp + Ms

    off_q, off_k, off_v, off_r = 0, GK, 2 * GK, 2 * GK + GV
    off_u = 2 * GK + 2 * GV
    off_sq = off_u + PD
    off_sk = off_sq + SQ
    off_sv = off_sk + SKV
    NZ = off_sv + SKV
    in_sizes = (GK, GK, GV, GV, rank, SQ, SKV, SKV, PD)
    in_off = [0]
    for s in in_sizes:
        in_off.append(in_off[-1] + s)
    segments = ((0, in_off[4]), (in_off[8], in_off[9]), (in_off[5], in_off[8]))

    tm = _pick_tile(M, 1056, BF16_SUBLANES)
    tm_half = _pick_tile(M, 528, BF16_SUBLANES)
    tf = _pick_tile(F, 256, LANES)
    tk_down = _pick_tile(F, 5504, LANES)
    tn_z = _pick_tile(NZ, 768, LANES)
    tn_d = _pick_tile(D, 512, LANES)
    tn_g = _pick_tile(D, 256, LANES)
    Tt_pool = _pick_tile(Tp, 256, POOL_HALO)
    M_tail = Bp * SUBLANES + Ms

    w_main_b, w_lr_b = _cast_in_proj(jnp.swapaxes(w_in, 1, 2), segments, (in_off[4], in_off[5]))
    w_br_b = [_cast_weights(w_branch_gla), _cast_weights(w_branch_swa), _cast_weights(w_branch_pool)]
    w_gate_b, w_out_b, w_up_b, w_down_b, w_pg_b = (
        _cast_weights(w, 0) for w in (w_gate, w_out, w_up, w_down, w_ple_gate))
    w_ple_b = w_ple.astype(BF16)
    w_a2_b = w_gla_a2.astype(BF16)
    w_pool_b = w_pool.astype(BF16)
    b_conv3 = b_conv.reshape(depth, 1, F)

    heads = jnp.arange(1, NH + 1, dtype=F32)
    slopes = jnp.exp2(-8.0 * heads / NH).reshape(KV, G, 1)

    qi = jnp.arange(WINDOW, dtype=jnp.int32)[:, None]
    kj = jnp.arange(2 * WINDOW, dtype=jnp.int32)[None, :]
    dist = qi + WINDOW - kj
    in_win = (dist >= 0) & (dist < WINDOW)
    has_key = jnp.stack([in_win & (kj >= WINDOW), in_win])
    swa_bias = jnp.where(has_key[:, None], -(slopes.reshape(NH, 1, 1) * dist.astype(F32))[None], -jnp.inf)

    def head_cols(v, tq):
        return jnp.broadcast_to(v[:, :, None, :], (KV, G, tq, 1)).reshape(KV, G * tq, 1)

    x = jnp.concatenate([x_prompt.reshape(Mp, D), x_sample.reshape(Ms, D)], axis=0)
    p_all = jnp.concatenate([p_prompt.reshape(depth, Mp, -1), p_sample.reshape(depth, Ms, -1)],
                            axis=1).astype(BF16)
    gla_p = jnp.zeros((depth, Bp, H, DK, DV), F32)
    gla_s = jnp.zeros((depth, Bs, H, DK, DV), F32)
    ya = jnp.zeros((M, GV), BF16)
    yb = jnp.zeros((M, SQ), BF16)
    yc = jnp.zeros((M, PD), BF16)
    tail_rows = [slice((b + 1) * Tp - SUBLANES, (b + 1) * Tp) for b in range(Bp)]
    outs_p, outs_s = [], []
    h, ssq = _norm_prep(x, g_mix[0])

    for i in range(depth):
        b_a = b_gla_a[i].reshape(1, GK)
        g_o = g_gla_out[i].reshape(1, GV)
        p_scale = pool_scale[i].reshape(1, PD)
        sinks = swa_sinks[i].reshape(KV, G, 1)

        nxt = i + 1 if i + 1 < depth else None

        def side(w):
            return None if nxt is None else (w, nxt)

        z = _fused_matmul([h], [(0, w_main_b, i, 0)], [], _plain, NZ, F32, tm, tn_z, "in_proj",
                          scale=(ssq, (0,)))
        zlr = _fused_matmul([h], [(0, w_lr_b, i, 0)], [], _plain, LANES, F32, tm, LANES, "in_proj_lr",
                            scale=(ssq, (0,)))

        gla_args = (w_a2_b[i], b_a, g_o)
        gla_dims = (H, DK, DV, off_q, off_k, off_v, off_r)
        ya, gla_p = _gla(z, zlr, *gla_args, None, ya, gla_p, i, 0, Bp, Tp, *gla_dims)
        ya, gla_s = _gla(z, zlr, *gla_args, state_gla, ya, gla_s, i, Mp, Bs, Ts, *gla_dims)

        swa_dims = (KV, G, dh, off_sq, off_sk, off_sv)
        yb = _swa_blocks(z, swa_bias, swa_sinks[i], yb, Bp, Tp, *swa_dims)
        ck = cache_swa_k[i].reshape(Bs, WB, SKV)
        cv = cache_swa_v[i].reshape(Bs, WB, SKV)
        yb = _swa(z, ck, cv, head_cols(slopes, Ts), head_cols(sinks, Ts), yb, Mp, Bs, Ts, Ts, *swa_dims)

        yc = _pool(z, None, w_pool_b[i], p_scale, yc, 0, Bp, Tp, Tt_pool, off_u)
        pool_prefix = jnp.pad(state_pool[i], ((0, 0), (POOL_HALO - P, 0), (0, 0)))
        yc = _pool(z, pool_prefix, w_pool_b[i], p_scale, yc, Mp, Bs, Ts, Ts, off_u)

        m = _fused_matmul([h, ya, yb, yc],
                          [(0, w_gate_b, None, 0), (1, w_br_b[0], i, 0), (0, w_gate_b, None, D),
                           (2, w_br_b[1], i, 0), (0, w_gate_b, None, 2 * D), (3, w_br_b[2], i, 0)],
                          [], _gate_combine, D, BF16, tm_half, tn_g, "gate_combine",
                          scale=(ssq, (0, 2, 4)), cast=side(w_gate))
        if nxt is not None:
            m, w_gate_n = m
        x, h, ssq, *w_out_n = _fused_matmul([m], [(0, w_out_b, None, 0)], [x], _residual, D, F32, tm, tn_d,
                                            "out_proj", norm_g=g_ffn[i], cast=side(w_out))

        pre = state_ffn_conv[i]
        r0 = jnp.broadcast_to(pre[:, 0:1], (Bs, Ts, F)).reshape(Ms, F)
        r1 = jnp.broadcast_to(pre[:, 1:2], (Bs, Ts, F)).reshape(Ms, F)
        act = _up_conv(h, ssq, w_up_b, w_conv, b_conv3, r0, r1, i, Mp, Tp, Ts, tm, tf, side(w_up))
        if nxt is not None:
            act, w_up_n = act
        h_tail = jnp.concatenate([h[r] for r in tail_rows] + [h[Mp:]], axis=0)
        ssq_tail = jnp.concatenate([ssq[r] for r in tail_rows] + [ssq[Mp:]], axis=0)
        a_tail = _fused_matmul([h_tail], [(0, w_up_b, None, 0)], [], _plain, F, F32, M_tail, tf, "up_proj_tail",
                               scale=(ssq_tail, (0,)))
        x, h, ssq, *w_down_n = _residual_matmul_ksplit(act, w_down_b, x, g_ple[i], tm, tn_d, tk_down,
                                                       "down_proj", side(w_down))

        res = _fused_matmul([h, p_all[i]], [(0, w_pg_b, None, 0), (1, w_ple_b, i, 0)], [x], _ple, D, F32,
                            tm, tn_d, "ple", scale=(ssq, (0,)),
                            norm_g=None if nxt is None else g_mix[nxt], cast=side(w_ple_gate))
        if nxt is None:
            x = res
        else:
            x, h, ssq, w_pg_n = res
            w_gate_b, w_out_b, w_up_b, w_down_b, w_pg_b = w_gate_n, w_out_n[0], w_up_n, w_down_n[0], w_pg_n

        wb = min(WINDOW, Tp)

        def prompt_tail(n_rows, off, width):
            return jnp.stack([z[(b + 1) * Tp - n_rows:(b + 1) * Tp, off:off + width] for b in range(Bp)])

        zs = z[Mp:].reshape(Bs, Ts, NZ)
        swa_k_p = prompt_tail(wb, off_sk, SKV).reshape(Bp, wb, KV, dh)
        swa_v_p = prompt_tail(wb, off_sv, SKV).reshape(Bp, wb, KV, dh)
        pool_p = prompt_tail(P, off_u, PD)
        conv_p = a_tail[:Bp * SUBLANES].reshape(Bp, SUBLANES, F)[:, SUBLANES - (CW - 1):]
        swa_k_s = jnp.concatenate([ck, zs[:, :, off_sk:off_sk + SKV]], axis=1)[:, Ts:].reshape(Bs, WB, KV, dh)
        swa_v_s = jnp.concatenate([cv, zs[:, :, off_sv:off_sv + SKV]], axis=1)[:, Ts:].reshape(Bs, WB, KV, dh)
        pool_s = jnp.concatenate([state_pool[i], zs[:, :, off_u:off_u + PD]], axis=1)[:, Ts:]
        conv_s = a_tail[Bp * SUBLANES:].reshape(Bs, Ts, F)[:, Ts - (CW - 1):]
        outs_p.append((swa_k_p, swa_v_p, pool_p, conv_p))
        outs_s.append((swa_k_s, swa_v_s, pool_s, conv_s))

    y_prompt = _rmsnorm(x, g_final, F32, 0, Mp).reshape(Bp, Tp, D)
    y_sample = _rmsnorm(x, g_final, F32, Mp, Ms).reshape(Bs, Ts, D)
    stack = lambda outs, c: jnp.stack([o[c] for o in outs])
    return (y_prompt, y_sample,
            gla_p, stack(outs_p, 0), stack(outs_p, 1), stack(outs_p, 2), stack(outs_p, 3),
            gla_s, stack(outs_s, 0), stack(outs_s, 1), stack(outs_s, 2), stack(outs_s, 3))
```

```python
import functools
import math

import jax
import jax.numpy as jnp
from jax import lax
from jax.experimental import pallas as pl
from jax.experimental.pallas import tpu as pltpu

EPS = 1e-6
GLA_TAU = 16.0
GLA_CHUNK = 64
WINDOW = 128
POOL_WINDOWS = (2, 4, 8, 16)

VMEM_LIMIT_BYTES = 56 * 1024 * 1024
LANES = 128
SUBLANES = 8
BF16_SUBLANES = 16
POOL_HALO = 16
CAST_BLOCK_BYTES = 6 * 1024 * 1024

F32 = jnp.float32
BF16 = jnp.bfloat16


def _pick_tile(n, target, align):
    best = None
    for t in range(align, min(n, target) + 1, align):
        if n % t == 0:
            best = t
    return n if best is None else best


def _params(*sem):
    return pltpu.CompilerParams(dimension_semantics=sem, vmem_limit_bytes=VMEM_LIMIT_BYTES)


def _cast_kernel(w_ref, o_ref):
    o_ref[...] = w_ref[...].astype(o_ref.dtype)


def _cast_weights(w, layer=None):
    L, K, N = w.shape
    tk = _pick_tile(K, max(BF16_SUBLANES, CAST_BLOCK_BYTES // (4 * N)), BF16_SUBLANES)
    if layer is None:
        spec = pl.BlockSpec((None, tk, N), lambda l, k: (l, k, 0))
        return pl.pallas_call(
            _cast_kernel, grid=(L, K // tk), in_specs=[spec], out_specs=spec,
            out_shape=jax.ShapeDtypeStruct((L, K, N), BF16),
            compiler_params=_params("parallel", "parallel"), name="cast_weights",
        )(w)
    return pl.pallas_call(
        _cast_kernel, grid=(K // tk,),
        in_specs=[pl.BlockSpec((None, tk, N), lambda k: (layer, k, 0))],
        out_specs=pl.BlockSpec((tk, N), lambda k: (k, 0)),
        out_shape=jax.ShapeDtypeStruct((K, N), BF16),
        compiler_params=_params("parallel"), name="cast_layer",
    )(w)


def _side_cast(src, layer, steps, step_of):
    _, K, N = src.shape
    rows = next(r for r in range(BF16_SUBLANES, K + 1, BF16_SUBLANES) if K % r == 0 and K // r <= steps)
    last = K // rows - 1
    in_spec = pl.BlockSpec((None, rows, N), lambda *g: (layer, jnp.minimum(step_of(*g), last), 0))
    out_spec = pl.BlockSpec((rows, N), lambda *g: (jnp.minimum(step_of(*g), last), 0))
    return in_spec, out_spec, jax.ShapeDtypeStruct((K, N), BF16)


def _weight_spec(w, layer, rows, cols, index):
    if w.ndim == 2:
        return pl.BlockSpec((rows, cols), index)
    return pl.BlockSpec((None, rows, cols), lambda *g: (layer,) + tuple(index(*g)))


def _cast_in_proj_kernel(wt_ref, main_ref, lr_ref, *, segments, lr):
    wt = wt_ref[...]
    main = jnp.concatenate([wt[a:b, :] for a, b in segments], axis=0)
    main_ref[...] = main.T.astype(main_ref.dtype)
    pad = jnp.zeros((lr_ref.shape[1] - (lr[1] - lr[0]), wt.shape[1]), F32)
    lr_ref[...] = jnp.concatenate([wt[lr[0]:lr[1], :], pad], axis=0).T.astype(lr_ref.dtype)


def _cast_in_proj(wt, segments, lr):
    L, N, K = wt.shape
    n_main = sum(b - a for a, b in segments)
    tk = _pick_tile(K, LANES, LANES)
    return pl.pallas_call(
        functools.partial(_cast_in_proj_kernel, segments=segments, lr=lr),
        grid=(L, K // tk),
        in_specs=[pl.BlockSpec((None, N, tk), lambda l, k: (l, 0, k))],
        out_specs=[pl.BlockSpec((None, tk, n_main), lambda l, k: (l, k, 0)),
                   pl.BlockSpec((None, tk, LANES), lambda l, k: (l, k, 0))],
        out_shape=[jax.ShapeDtypeStruct((L, K, n_main), BF16),
                   jax.ShapeDtypeStruct((L, K, LANES), BF16)],
        compiler_params=_params("parallel", "parallel"), name="cast_in_proj",
    )(wt)


def _rmsnorm_kernel(x_ref, g_ref, o_ref):
    x = x_ref[...]
    y = x * lax.rsqrt(jnp.mean(x * x, axis=-1, keepdims=True) + EPS)
    o_ref[...] = (y * g_ref[...]).astype(o_ref.dtype)


def _rmsnorm(x, g, out_dtype, row0=0, rows=None):
    M, D = x.shape
    rows = M if rows is None else rows
    tm = _pick_tile(math.gcd(rows, row0) if row0 else rows, 512, BF16_SUBLANES)
    rb0 = row0 // tm
    return pl.pallas_call(
        _rmsnorm_kernel,
        grid=(rows // tm,),
        in_specs=[pl.BlockSpec((tm, D), lambda i: (rb0 + i, 0)),
                  pl.BlockSpec((1, D), lambda i: (0, 0))],
        out_specs=pl.BlockSpec((tm, D), lambda i: (i, 0)),
        out_shape=jax.ShapeDtypeStruct((rows, D), out_dtype),
        compiler_params=_params("parallel"),
        name="rmsnorm",
    )(x, g.reshape(1, D))


def _norm_prep_kernel(x_ref, g_ref, xg_ref, ssq_ref):
    x = x_ref[...]
    xg_ref[...] = (x * g_ref[...]).astype(xg_ref.dtype)
    ssq_ref[...] = jnp.sum(x * x, axis=-1, keepdims=True)


def _norm_prep(x, g):
    M, D = x.shape
    tm = _pick_tile(M, 512, BF16_SUBLANES)
    return pl.pallas_call(
        _norm_prep_kernel,
        grid=(M // tm,),
        in_specs=[pl.BlockSpec((tm, D), lambda i: (i, 0)), pl.BlockSpec((1, D), lambda i: (0, 0))],
        out_specs=[pl.BlockSpec((tm, D), lambda i: (i, 0)), pl.BlockSpec((tm, 1), lambda i: (i, 0))],
        out_shape=[jax.ShapeDtypeStruct((M, D), BF16), jax.ShapeDtypeStruct((M, 1), F32)],
        compiler_params=_params("parallel"),
        name="norm_prep",
    )(x, g.reshape(1, D))


def _row_scale(ssq_ref, d_model):
    return lax.rsqrt(ssq_ref[...] / d_model + EPS)


def _emit_norm_parts(val, g_ref, xg_ref, ssq_ref, j):
    xg_ref[...] = (val * g_ref[...]).astype(xg_ref.dtype)
    part = jnp.sum(val * val, axis=-1, keepdims=True)

    @pl.when(j == 0)
    def _():
        ssq_ref[...] = part

    @pl.when(j > 0)
    def _():
        ssq_ref[...] += part


def _mm_kernel(*refs, n_acts, pair_act, n_extra, combine, scaled, d_model, norm_out, has_cast):
    it = iter(refs)
    act_refs = [next(it) for _ in range(n_acts)]
    w_refs = [next(it) for _ in pair_act]
    extras = [next(it) for _ in range(n_extra)]
    ssq_in = next(it) if scaled else None
    g_ref = next(it) if norm_out else None
    cast_src = next(it) if has_cast else None
    o_ref = next(it)
    xg_ref, ssq_ref = (next(it), next(it)) if norm_out else (None, None)
    if has_cast:
        next(it)[...] = cast_src[...].astype(BF16)
    r = _row_scale(ssq_in, d_model) if scaled else None

    def dot(p):
        d = jnp.dot(act_refs[pair_act[p]][...], w_refs[p][...], preferred_element_type=F32)
        return d * r if p in scaled else d

    val = combine(dot, extras)
    o_ref[...] = val.astype(o_ref.dtype)
    if norm_out:
        _emit_norm_parts(val, g_ref, xg_ref, ssq_ref, pl.program_id(1))


def _fused_matmul(acts, pairs, extras, combine, n_out, out_dtype, tm, tn, name,
                  scale=None, norm_g=None, cast=None):
    M = acts[0].shape[0]
    n_j = n_out // tn
    in_specs = [pl.BlockSpec((tm, a.shape[1]), lambda i, j: (i, 0)) for a in acts]
    args = list(acts)
    for ai, w, layer, off in pairs:
        K = acts[ai].shape[1]
        assert off % tn == 0 and w.shape[-2] == K
        in_specs.append(_weight_spec(w, layer, K, tn, lambda i, j, ob=off // tn: (0, j + ob)))
        args.append(w)
    for e in extras:
        in_specs.append(pl.BlockSpec((tm, tn), lambda i, j: (i, j)))
        args.append(e)
    out_specs = [pl.BlockSpec((tm, tn), lambda i, j: (i, j))]
    out_shape = [jax.ShapeDtypeStruct((M, n_out), out_dtype)]
    scaled, d_model = (), None
    if scale is not None:
        scaled = tuple(scale[1])
        d_model = acts[pairs[scaled[0]][0]].shape[1]
        in_specs.append(pl.BlockSpec((tm, 1), lambda i, j: (i, 0)))
        args.append(scale[0])
    if norm_g is not None:
        in_specs.append(pl.BlockSpec((1, tn), lambda i, j: (0, j)))
        args.append(norm_g.reshape(1, n_out))
        out_specs += [pl.BlockSpec((tm, tn), lambda i, j: (i, j)), pl.BlockSpec((tm, 1), lambda i, j: (i, 0))]
        out_shape += [jax.ShapeDtypeStruct((M, n_out), BF16), jax.ShapeDtypeStruct((M, 1), F32)]
    if cast is not None:
        c_in, c_out, c_shape = _side_cast(cast[0], cast[1], (M // tm) * n_j, lambda i, j: i * n_j + j)
        in_specs.append(c_in)
        args.append(cast[0])
        out_specs.append(c_out)
        out_shape.append(c_shape)
    res = pl.pallas_call(
        functools.partial(_mm_kernel, n_acts=len(acts), pair_act=tuple(p[0] for p in pairs),
                          n_extra=len(extras), combine=combine, scaled=scaled, d_model=d_model,
                          norm_out=norm_g is not None, has_cast=cast is not None),
        grid=(M // tm, n_j),
        in_specs=in_specs,
        out_specs=out_specs,
        out_shape=out_shape,
        compiler_params=_params("arbitrary", "arbitrary"),
        name=name,
    )(*args)
    return res[0] if len(res) == 1 else tuple(res)


def _mm_ksplit_kernel(*refs, nk, has_cast):
    a_ref, w_ref, x_ref, g_ref = refs[:4]
    o_ref, xg_ref, ssq_ref = refs[4 + has_cast:7 + has_cast]
    acc_ref = refs[-1]
    if has_cast:
        refs[8][...] = refs[4][...].astype(BF16)
    k = pl.program_id(2)
    d = jnp.dot(a_ref[...], w_ref[...], preferred_element_type=F32)

    @pl.when(k == 0)
    def _():
        acc_ref[...] = d

    @pl.when(k > 0)
    def _():
        acc_ref[...] += d

    @pl.when(k == nk - 1)
    def _():
        val = x_ref[...] + acc_ref[...]
        o_ref[...] = val
        _emit_norm_parts(val, g_ref, xg_ref, ssq_ref, pl.program_id(1))


def _residual_matmul_ksplit(a, w, x, norm_g, tm, tn, tk, name, cast=None):
    M, K = a.shape
    N = w.shape[1]
    nk, n_j = K // tk, N // tn
    in_specs = [pl.BlockSpec((tm, tk), lambda i, j, k: (i, k)),
                pl.BlockSpec((tk, tn), lambda i, j, k: (k, j)),
                pl.BlockSpec((tm, tn), lambda i, j, k: (i, j)),
                pl.BlockSpec((1, tn), lambda i, j, k: (0, j))]
    out_specs = [pl.BlockSpec((tm, tn), lambda i, j, k: (i, j)),
                 pl.BlockSpec((tm, tn), lambda i, j, k: (i, j)),
                 pl.BlockSpec((tm, 1), lambda i, j, k: (i, 0))]
    out_shape = [jax.ShapeDtypeStruct((M, N), F32), jax.ShapeDtypeStruct((M, N), BF16),
                 jax.ShapeDtypeStruct((M, 1), F32)]
    args = [a, w, x, norm_g.reshape(1, N)]
    if cast is not None:
        c_in, c_out, c_shape = _side_cast(cast[0], cast[1], (M // tm) * n_j * nk,
                                          lambda i, j, k: (i * n_j + j) * nk + k)
        in_specs.append(c_in)
        args.append(cast[0])
        out_specs.append(c_out)
        out_shape.append(c_shape)
    res = pl.pallas_call(
        functools.partial(_mm_ksplit_kernel, nk=nk, has_cast=cast is not None),
        grid=(M // tm, n_j, nk),
        in_specs=in_specs,
        out_specs=out_specs,
        out_shape=out_shape,
        scratch_shapes=[pltpu.VMEM((tm, tn), F32)],
        compiler_params=_params("arbitrary", "arbitrary", "arbitrary"),
        name=name,
    )(*args)
    return tuple(res)


GELU_K0 = -2.0 * math.sqrt(2.0 / math.pi) * math.log2(math.e)
GELU_K1 = 0.044715 * GELU_K0


def _conv_gelu(a, a1, a2, wc_ref, bc_ref):
    x = bc_ref[...] + a2 * wc_ref[0:1, :]
    x = x + a1 * wc_ref[1:2, :]
    x = x + a * wc_ref[2:3, :]
    return x / (1.0 + jnp.exp2(x * (GELU_K0 + GELU_K1 * (x * x))))


def _up_conv_kernel(*refs, tm, Mp, Tp, Ts, Ms, piece, has_cast):
    h_ref, ssq_ref, wa_ref, wg_ref, wc_ref, bc_ref, r0_ref, r1_ref = refs[:8]
    o_ref, tail_s_ref, tail_p_ref = refs[8 + has_cast:11 + has_cast]
    a_scr, g_scr, halo_scr = refs[-3:]
    if has_cast:
        refs[12][...] = refs[8][...].astype(BF16)
    i, j = pl.program_id(0), pl.program_id(1)

    @pl.when(i == 0)
    def _():
        halo_scr[j] = jnp.zeros(halo_scr.shape[1:], F32)

    h = h_ref[...]
    r = _row_scale(ssq_ref, h.shape[1])
    a_scr[0:SUBLANES, :] = halo_scr[j]
    a_scr[SUBLANES:, :] = jnp.dot(h, wa_ref[...], preferred_element_type=F32) * r
    g_scr[...] = jnp.dot(h, wg_ref[...], preferred_element_type=F32) * r
    halo_scr[j] = a_scr[tm:, :]

    row0 = i * tm
    tail_s_ref[...] = a_scr[SUBLANES + tm - Ms:, :]
    seq_end = (row0 // Tp + 1) * Tp
    off = pl.multiple_of(jnp.clip(seq_end - SUBLANES - row0, 0, tm - SUBLANES), SUBLANES)
    tail_p_ref[...] = a_scr[pl.ds(SUBLANES + off, SUBLANES), :]

    sub = lax.broadcasted_iota(jnp.int32, (SUBLANES, 1), 0)
    for lo in range(0, tm, piece):
        ext = a_scr[lo:lo + SUBLANES + piece, :]
        a = ext[SUBLANES:, :]
        a1 = pltpu.roll(ext, 1, 0)[SUBLANES:, :]
        a2 = pltpu.roll(ext, 2, 0)[SUBLANES:, :]
        if lo < tm - Ms:
            starts = (row0 + lo) % Tp == 0
            top1 = jnp.where(starts & (sub == 0), 0.0, a1[:SUBLANES, :])
            top2 = jnp.where(starts & (sub < 2), 0.0, a2[:SUBLANES, :])
            a1 = jnp.concatenate([top1, a1[SUBLANES:, :]], axis=0)
            a2 = jnp.concatenate([top2, a2[SUBLANES:, :]], axis=0)
        else:
            row = row0 + lo + lax.broadcasted_iota(jnp.int32, (piece, 1), 0)
            is_sample = row >= Mp
            pos = jnp.where(is_sample, (row - Mp) % Ts, row % Tp)
            ls = lo - (tm - Ms)
            hist2 = jnp.where(is_sample, r0_ref[ls:ls + piece, :], 0.0)
            hist1 = jnp.where(is_sample, r1_ref[ls:ls + piece, :], 0.0)
            a1 = jnp.where(pos == 0, hist1, a1)
            a2 = jnp.where(pos == 0, hist2, jnp.where(pos == 1, hist1, a2))
        act = _conv_gelu(a, a1, a2, wc_ref, bc_ref) * g_scr[lo:lo + piece, :]
        o_ref[lo:lo + piece, :] = act.astype(o_ref.dtype)


def _up_conv(h, ssq, w_up, w_conv, b_conv, r0, r1, layer, Mp, Tp, Ts, tm, tf, cast=None):
    M, K = h.shape
    F = w_conv.shape[2]
    Ms = r0.shape[0]
    n_i, n_f = M // tm, F // tf
    piece = next(p for p in (2 * BF16_SUBLANES, BF16_SUBLANES) if tm % p == 0 and (tm - Ms) % p == 0)
    assert M % tm == 0 and F % tf == 0 and M - Mp == Ms and Ms <= tm <= Tp
    assert tm % piece == 0 and (tm - Ms) % piece == 0 and Tp % piece == 0
    in_specs = [pl.BlockSpec((tm, K), lambda i, j: (i, 0)),
                pl.BlockSpec((tm, 1), lambda i, j: (i, 0)),
                pl.BlockSpec((K, tf), lambda i, j: (0, j)),
                pl.BlockSpec((K, tf), lambda i, j: (0, n_f + j)),
                pl.BlockSpec((None, w_conv.shape[1], tf), lambda i, j: (layer, 0, j)),
                pl.BlockSpec((None, 1, tf), lambda i, j: (layer, 0, j)),
                pl.BlockSpec((Ms, tf), lambda i, j: (0, j)),
                pl.BlockSpec((Ms, tf), lambda i, j: (0, j))]
    out_specs = [pl.BlockSpec((tm, tf), lambda i, j: (i, j)),
                 pl.BlockSpec((None, Ms, tf), lambda i, j: (i, 0, j)),
                 pl.BlockSpec((None, SUBLANES, tf), lambda i, j: (i, 0, j))]
    out_shape = [jax.ShapeDtypeStruct((M, F), BF16), jax.ShapeDtypeStruct((n_i, Ms, F), F32),
                 jax.ShapeDtypeStruct((n_i, SUBLANES, F), F32)]
    args = [h, ssq, w_up, w_up, w_conv, b_conv, r0, r1]
    if cast is not None:
        c_in, c_out, c_shape = _side_cast(cast[0], cast[1], n_i * n_f, lambda i, j: i * n_f + j)
        in_specs.append(c_in)
        args.append(cast[0])
        out_specs.append(c_out)
        out_shape.append(c_shape)
    res = pl.pallas_call(
        functools.partial(_up_conv_kernel, tm=tm, Mp=Mp, Tp=Tp, Ts=Ts, Ms=Ms, piece=piece,
                          has_cast=cast is not None),
        grid=(n_i, n_f),
        in_specs=in_specs,
        out_specs=out_specs,
        out_shape=out_shape,
        scratch_shapes=[pltpu.VMEM((SUBLANES + tm, tf), F32), pltpu.VMEM((tm, tf), F32),
                        pltpu.VMEM((n_f, SUBLANES, tf), F32)],
        compiler_params=_params("arbitrary", "arbitrary"),
        name="up_conv",
    )(*args)
    return tuple(res)


def _gla_chunk(q, k, v, glr, w_a2, b_a, S, C):
    DK = q.shape[1]
    la = jnp.dot(glr.astype(BF16), w_a2, preferred_element_type=F32) + b_a
    log_a = (jnp.minimum(la, 0.0) - jnp.log1p(jnp.exp(-jnp.abs(la)))) / GLA_TAU
    row = lax.broadcasted_iota(jnp.int32, (C, DK), 0)
    b = log_a
    s = 1
    while s < C:
        b = b + jnp.where(row >= s, pltpu.roll(b, s, 0), 0.0)
        s *= 2
    b_last = b[C - 1:C, :]
    q_t = ((q * (DK ** -0.5)) * jnp.exp(b)).astype(BF16)
    k_t = (k * jnp.exp(-b)).astype(BF16)
    att = lax.dot_general(q_t, k_t, (((1,), (1,)), ((), ())), preferred_element_type=F32)
    ri = lax.broadcasted_iota(jnp.int32, (C, C), 0)
    ci = lax.broadcasted_iota(jnp.int32, (C, C), 1)
    att = jnp.where(ri >= ci, att, 0.0)
    vb = v.astype(BF16)
    o = (jnp.dot(att.astype(BF16), vb, preferred_element_type=F32)
         + jnp.dot(q_t, S.astype(BF16), preferred_element_type=F32))
    k_end = (k * jnp.exp(b_last - b)).astype(BF16)
    er = lax.broadcasted_iota(jnp.int32, (DK, DK), 0)
    ec = lax.broadcasted_iota(jnp.int32, (DK, DK), 1)
    decay_col = jnp.sum(jnp.where(er == ec, jnp.exp(b_last), 0.0), axis=1, keepdims=True)
    S_new = decay_col * S + lax.dot_general(k_end, vb, (((0,), (0,)), ((), ())),
                                            preferred_element_type=F32)
    return o, S_new


def _gla_kernel(*refs, C, nb, H, DK, DV, rank, n_chunks, has_s0):
    q_ref, k_ref, v_ref, r_ref, lr_ref, wa_ref, ba_ref, go_ref = refs[:8]
    s0_ref = refs[8] if has_s0 else None
    y_ref, sout_ref, s_scr = refs[-3:]
    n = pl.program_id(1)

    @pl.when(n == 0)
    def _():
        s_scr[...] = s0_ref[...] if has_s0 else jnp.zeros(s_scr.shape, F32)

    per_seq = []
    for j in range(nb):
        rows = slice(j * C, (j + 1) * C)
        glr = lr_ref[rows, :rank]
        heads = []
        for h in range(H):
            kc, vc = slice(h * DK, (h + 1) * DK), slice(h * DV, (h + 1) * DV)
            o, s_new = _gla_chunk(q_ref[rows, kc], k_ref[rows, kc], v_ref[rows, vc], glr,
                                  wa_ref[:, kc], ba_ref[:, kc], s_scr[j, h], C)
            s_scr[j, h] = s_new
            o = o * lax.rsqrt(jnp.mean(o * o, axis=-1, keepdims=True) + EPS) * go_ref[:, vc]
            r = r_ref[rows, vc]
            heads.append(o * (r * jax.nn.sigmoid(r)))
        per_seq.append(jnp.concatenate(heads, axis=1))
    y = per_seq[0] if nb == 1 else jnp.concatenate(per_seq, axis=0)
    y_ref[...] = y.astype(y_ref.dtype)

    @pl.when(n == n_chunks - 1)
    def _():
        sout_ref[...] = s_scr[...]


def _gla(z, zlr, w_a2, b_a, g_out, s0, ybuf, sbuf, layer, row0, B, T, H, DK, DV, off_q, off_k, off_v, off_r):
    rank = w_a2.shape[0]
    GK, GV = H * DK, H * DV
    C = math.gcd(T, GLA_CHUNK)
    n_chunks = T // C
    nb = 1 if C % BF16_SUBLANES == 0 else BF16_SUBLANES // C
    assert nb == 1 or n_chunks == 1
    assert B % nb == 0 and row0 % (nb * C) == 0
    assert off_q % GK == 0 and off_k % GK == 0 and off_v % GV == 0 and off_r % GV == 0
    R = nb * C
    rb0 = row0 // R

    def zspec(width, off):
        return pl.BlockSpec((R, width), lambda b, n: (rb0 + b * n_chunks + n, off // width))

    state_spec = pl.BlockSpec((None, nb, H, DK, DV), lambda b, n: (layer, b, 0, 0, 0))
    full = lambda a: pl.BlockSpec(a.shape, lambda b, n: (0,) * a.ndim)
    any_spec = pl.BlockSpec(memory_space=pl.ANY)
    in_specs = [zspec(GK, off_q), zspec(GK, off_k), zspec(GV, off_v), zspec(GV, off_r),
                pl.BlockSpec((R, zlr.shape[1]), lambda b, n: (rb0 + b * n_chunks + n, 0)),
                full(w_a2), full(b_a), full(g_out)]
    args = [z, z, z, z, zlr, w_a2, b_a, g_out]
    if s0 is not None:
        in_specs.append(state_spec)
        args.append(s0)
    n_in = len(args)
    return pl.pallas_call(
        functools.partial(_gla_kernel, C=C, nb=nb, H=H, DK=DK, DV=DV, rank=rank, n_chunks=n_chunks,
                          has_s0=s0 is not None),
        grid=(B // nb, n_chunks),
        in_specs=in_specs + [any_spec, any_spec],
        out_specs=[pl.BlockSpec((R, GV), lambda b, n: (rb0 + b * n_chunks + n, 0)), state_spec],
        out_shape=[jax.ShapeDtypeStruct(ybuf.shape, ybuf.dtype),
                   jax.ShapeDtypeStruct(sbuf.shape, sbuf.dtype)],
        scratch_shapes=[pltpu.VMEM((nb, H, DK, DV), F32)],
        input_output_aliases={n_in: 0, n_in + 1: 1},
        compiler_params=_params("parallel", "arbitrary"),
        name="gla",
    )(*args, ybuf, sbuf)


def _swa_block_kernel(q_ref, k_ref, v_ref, kp_ref, vp_ref, bias_ref, sink_ref, ybuf_ref, y_ref,
                      *, KV, G, dh):
    del ybuf_ref
    scale = dh ** -0.5
    fold_scale = math.frexp(scale)[0] == 0.5
    heads = []
    for kv in range(KV):
        cs = slice(kv * dh, (kv + 1) * dh)
        kk = jnp.concatenate([kp_ref[:, cs], k_ref[:, cs]], axis=0).astype(BF16)
        vv = jnp.concatenate([vp_ref[:, cs], v_ref[:, cs]], axis=0).astype(BF16)
        for g in range(G):
            h = kv * G + g
            q = q_ref[:, h * dh:(h + 1) * dh]
            q = (q * scale if fold_scale else q).astype(BF16)
            s = lax.dot_general(q, kk, (((1,), (1,)), ((), ())), preferred_element_type=F32)
            s = (s if fold_scale else s * scale) + bias_ref[h]
            sink = sink_ref[h]
            m = jnp.maximum(jnp.max(s, axis=1, keepdims=True), sink)
            e = jnp.exp(s - m)
            denom = jnp.sum(e, axis=1, keepdims=True) + jnp.exp(sink - m)
            heads.append(jnp.dot((e / denom).astype(BF16), vv, preferred_element_type=F32))
    y_ref[...] = jnp.concatenate(heads, axis=1).astype(y_ref.dtype)


def _swa_blocks(z, bias, sinks, ybuf, B, T, KV, G, dh, off_q, off_k, off_v):
    nq = T // WINDOW
    QW, KW = KV * G * dh, KV * dh
    assert T % WINDOW == 0 and off_q % QW == 0 and off_k % KW == 0 and off_v % KW == 0

    def own(width, off):
        return pl.BlockSpec((WINDOW, width), lambda b, n: (b * nq + n, off // width))

    def past(off):
        return pl.BlockSpec((WINDOW, KW), lambda b, n: (jnp.maximum(b * nq + n - 1, 0), off // KW))

    return pl.pallas_call(
        functools.partial(_swa_block_kernel, KV=KV, G=G, dh=dh),
        grid=(B, nq),
        in_specs=[own(QW, off_q), own(KW, off_k), own(KW, off_v), past(off_k), past(off_v),
                  pl.BlockSpec((None,) + bias.shape[1:], lambda b, n: (jnp.minimum(n, 1), 0, 0, 0)),
                  pl.BlockSpec(memory_space=pltpu.SMEM),
                  pl.BlockSpec(memory_space=pl.ANY)],
        out_specs=pl.BlockSpec((WINDOW, QW), lambda b, n: (b * nq + n, 0)),
        out_shape=jax.ShapeDtypeStruct(ybuf.shape, ybuf.dtype),
        input_output_aliases={7: 0},
        compiler_params=_params("parallel", "arbitrary"),
        name="swa_blocks",
    )(z, z, z, z, z, bias, sinks, ybuf)


def _swa_kernel(q_ref, k_ref, v_ref, kp_ref, vp_ref, slope_ref, sink_ref, ybuf_ref, y_ref,
                *, Tq, nb, KV, G, dh, first_block_has_no_past):
    del ybuf_ref
    n = pl.program_id(1)
    W = WINDOW
    rows, keys = G * Tq, W + Tq
    ri = lax.broadcasted_iota(jnp.int32, (rows, keys), 0)
    ci = lax.broadcasted_iota(jnp.int32, (rows, keys), 1)
    dist = (ri % Tq) + W - ci
    valid = (dist >= 0) & (dist < WINDOW)
    if first_block_has_no_past:
        valid = valid & ((ci >= W) | (n > 0))
    distf = dist.astype(F32)
    per_seq = []
    for j in range(nb):
        rs = slice(j * Tq, (j + 1) * Tq)
        q, k_own, v_own = q_ref[rs, :], k_ref[rs, :], v_ref[rs, :]
        k_past = kp_ref[j] if len(kp_ref.shape) == 3 else kp_ref[...]
        v_past = vp_ref[j] if len(vp_ref.shape) == 3 else vp_ref[...]
        heads = []
        for kv in range(KV):
            cs = slice(kv * dh, (kv + 1) * dh)
            kk = jnp.concatenate([k_past[:, cs], k_own[:, cs]], axis=0).astype(BF16)
            vv = jnp.concatenate([v_past[:, cs], v_own[:, cs]], axis=0).astype(BF16)
            qs = jnp.concatenate(
                [q[:, (kv * G + g) * dh:(kv * G + g + 1) * dh] for g in range(G)], axis=0).astype(BF16)
            s = lax.dot_general(qs, kk, (((1,), (1,)), ((), ())), preferred_element_type=F32)
            s = s * (dh ** -0.5) - slope_ref[kv] * distf
            s = jnp.where(valid, s, -jnp.inf)
            sink = sink_ref[kv]
            m = jnp.maximum(jnp.max(s, axis=1, keepdims=True), sink)
            e = jnp.exp(s - m)
            denom = jnp.sum(e, axis=1, keepdims=True) + jnp.exp(sink - m)
            o = jnp.dot((e / denom).astype(BF16), vv, preferred_element_type=F32)
            heads += [o[g * Tq:(g + 1) * Tq, :] for g in range(G)]
        per_seq.append(jnp.concatenate(heads, axis=1))
    y = per_seq[0] if nb == 1 else jnp.concatenate(per_seq, axis=0)
    y_ref[...] = y.astype(y_ref.dtype)


def _swa(z, k_past, v_past, slope_col, sink_col, ybuf, row0, B, T, Tq, KV, G, dh, off_q, off_k, off_v):
    nq = T // Tq
    nb = 1 if Tq % BF16_SUBLANES == 0 else BF16_SUBLANES // Tq
    assert nb == 1 or nq == 1
    R = nb * Tq
    assert B % nb == 0 and row0 % R == 0
    rb0 = row0 // R
    QW, KW = KV * G * dh, KV * dh
    assert off_q % QW == 0 and off_k % KW == 0 and off_v % KW == 0

    def own(width, off):
        return pl.BlockSpec((R, width), lambda b, n: (rb0 + b * nq + n, off // width))

    if k_past is None:
        assert Tq == WINDOW and nb == 1

        def past(off):
            return pl.BlockSpec((WINDOW, KW), lambda b, n: (jnp.maximum(rb0 + b * nq + n - 1, 0), off // KW))

        past_specs, past_args = [past(off_k), past(off_v)], [z, z]
    else:
        spec = pl.BlockSpec((nb, WINDOW, KW), lambda b, n: (b, 0, 0))
        past_specs, past_args = [spec, spec], [k_past, v_past]

    const = pl.BlockSpec((KV, G * Tq, 1), lambda b, n: (0, 0, 0))
    return pl.pallas_call(
        functools.partial(_swa_kernel, Tq=Tq, nb=nb, KV=KV, G=G, dh=dh,
                          first_block_has_no_past=k_past is None),
        grid=(B // nb, nq),
        in_specs=[own(QW, off_q), own(KW, off_k), own(KW, off_v)] + past_specs
                 + [const, const, pl.BlockSpec(memory_space=pl.ANY)],
        out_specs=pl.BlockSpec((R, QW), lambda b, n: (rb0 + b * nq + n, 0)),
        out_shape=jax.ShapeDtypeStruct(ybuf.shape, ybuf.dtype),
        input_output_aliases={7: 0},
        compiler_params=_params("parallel", "arbitrary"),
        name="swa",
    )(z, z, z, *past_args, slope_col, sink_col, ybuf)


def _pool_kernel(u_ref, halo_ref, w_ref, sc_ref, ybuf_ref, y_ref,
                 *, Tt, nb, P, G, Cg, first_block_has_no_past):
    del ybuf_ref
    n = pl.program_id(1)
    t_glob = n * Tt + lax.broadcasted_iota(jnp.int32, (Tt, 1), 0)
    per_seq = []
    for j in range(nb):
        u = u_ref[j * Tt:(j + 1) * Tt, :]
        halo = halo_ref[j] if len(halo_ref.shape) == 3 else halo_ref[...]
        if first_block_has_no_past:
            halo = jnp.where(n > 0, halo, 0.0)
        ext = jnp.concatenate([halo, u], axis=0)
        groups = []
        for g, w in enumerate(POOL_WINDOWS):
            cs = slice(g * Cg, (g + 1) * Cg)
            s = ext[:, cs]
            d = 1
            while d < w:
                s = s + pltpu.roll(s, d, 0)
                d *= 2
            count = jnp.minimum(w, P + 1 + t_glob).astype(F32)
            y = s[POOL_HALO:, :] / count - u[:, cs]
            groups.append(jnp.dot(y.astype(BF16), w_ref[g], preferred_element_type=F32))
        per_seq.append(jnp.concatenate(groups, axis=1) * sc_ref[...])
    y = per_seq[0] if nb == 1 else jnp.concatenate(per_seq, axis=0)
    y_ref[...] = y.astype(y_ref.dtype)


def _pool(z, prefix, w_pool, pool_scale, ybuf, row0, B, T, Tt, off_u):
    G, Cg, _ = w_pool.shape
    PD = G * Cg
    nt = T // Tt
    nb = 1 if Tt % BF16_SUBLANES == 0 else BF16_SUBLANES // Tt
    assert nb == 1 or nt == 1
    R = nb * Tt
    assert B % nb == 0 and row0 % R == 0 and off_u % PD == 0
    assert Tt % POOL_HALO == 0 or prefix is not None
    rb0 = row0 // R
    if prefix is None:
        hb = Tt // POOL_HALO
        halo_spec = pl.BlockSpec((POOL_HALO, PD),
                                 lambda b, n: (jnp.maximum((rb0 + b * nt + n) * hb - 1, 0), off_u // PD))
        halo_arg, P = z, 0
    else:
        halo_spec = pl.BlockSpec((nb, POOL_HALO, PD), lambda b, n: (b, 0, 0))
        halo_arg, P = prefix, POOL_HALO - 1
    return pl.pallas_call(
        functools.partial(_pool_kernel, Tt=Tt, nb=nb, P=P, G=G, Cg=Cg,
                          first_block_has_no_past=prefix is None),
        grid=(B // nb, nt),
        in_specs=[pl.BlockSpec((R, PD), lambda b, n: (rb0 + b * nt + n, off_u // PD)),
                  halo_spec,
                  pl.BlockSpec((G, Cg, Cg), lambda b, n: (0, 0, 0)),
                  pl.BlockSpec((1, PD), lambda b, n: (0, 0)),
                  pl.BlockSpec(memory_space=pl.ANY)],
        out_specs=pl.BlockSpec((R, PD), lambda b, n: (rb0 + b * nt + n, 0)),
        out_shape=jax.ShapeDtypeStruct(ybuf.shape, ybuf.dtype),
        input_output_aliases={4: 0},
        compiler_params=_params("parallel", "arbitrary"),
        name="pool",
    )(z, halo_arg, w_pool, pool_scale, ybuf)


def _plain(dot, extras):
    return dot(0)


def _gate_combine(dot, extras):
    acc = jax.nn.sigmoid(dot(0)) * dot(1)
    acc = acc + jax.nn.sigmoid(dot(2)) * dot(3)
    return acc + jax.nn.sigmoid(dot(4)) * dot(5)


def _residual(dot, extras):
    return extras[0][...] + dot(0)


def _ple(dot, extras):
    return extras[0][...] + dot(1) * jax.nn.sigmoid(dot(0))


def kernel(x_prompt, x_sample, state_gla, cache_swa_k, cache_swa_v, state_pool, state_ffn_conv, p_prompt, p_sample, g_mix, w_in, w_gla_a2, b_gla_a, g_gla_out, swa_sinks, w_pool, pool_scale, w_branch_gla, w_branch_swa, w_branch_pool, w_gate, w_out, g_ffn, w_up, w_conv, b_conv, w_down, g_ple, w_ple, w_ple_gate, g_final):
    Bp, Tp, D = x_prompt.shape
    Bs, Ts, _ = x_sample.shape
    depth = w_in.shape[0]
    _, _, H, DK, DV = state_gla.shape
    rank = w_gla_a2.shape[1]
    _, _, WB, KV, dh = cache_swa_k.shape
    NH = swa_sinks.shape[1]
    G = NH // KV
    _, PG, Cg, _ = w_pool.shape
    PD = PG * Cg
    P = state_pool.shape[2]
    F = w_down.shape[1]
    CW = w_conv.shape[1]
    assert WB == WINDOW and P == POOL_HALO - 1 and CW == 3 and PG == len(POOL_WINDOWS)
    assert Tp >= WINDOW and Tp >= P and Ts >= CW - 1
    GK, GV, SQ, SKV = H * DK, H * DV, NH * dh, KV * dh
    Mp, Ms = Bp * Tp, Bs * Ts
    M = Mp + Ms

    off_q, off_k, off_v, off_r = 0, GK, 2 * GK, 2 * GK + GV
    off_u = 2 * GK + 2 * GV
    off_sq = off_u + PD
    off_sk = off_sq + SQ
    off_sv = off_sk + SKV
    NZ = off_sv + SKV
    in_sizes = (GK, GK, GV, GV, rank, SQ, SKV, SKV, PD)
    in_off = [0]
    for s in in_sizes:
        in_off.append(in_off[-1] + s)
    segments = ((0, in_off[4]), (in_off[8], in_off[9]), (in_off[5], in_off[8]))

    tm = _pick_tile(M, 1056, BF16_SUBLANES)
    tm_half = _pick_tile(M, 528, BF16_SUBLANES)
    tf = _pick_tile(F, 256, LANES)
    tk_down = _pick_tile(F, 5504, LANES)
    tn_z = _pick_tile(NZ, 768, LANES)
    tn_d = _pick_tile(D, 512, LANES)
    tn_g = _pick_tile(D, 256, LANES)
    Tt_pool = _pick_tile(Tp, 256, POOL_HALO)

    w_main_b, w_lr_b = _cast_in_proj(jnp.swapaxes(w_in, 1, 2), segments, (in_off[4], in_off[5]))
    w_br_b = [_cast_weights(w_branch_gla), _cast_weights(w_branch_swa), _cast_weights(w_branch_pool)]
    w_gate_b, w_out_b, w_up_b, w_down_b, w_pg_b = (
        _cast_weights(w, 0) for w in (w_gate, w_out, w_up, w_down, w_ple_gate))
    w_ple_b = w_ple.astype(BF16)
    w_a2_b = w_gla_a2.astype(BF16)
    w_pool_b = w_pool.astype(BF16)
    b_conv3 = b_conv.reshape(depth, 1, F)

    heads = jnp.arange(1, NH + 1, dtype=F32)
    slopes = jnp.exp2(-8.0 * heads / NH).reshape(KV, G, 1)

    qi = jnp.arange(WINDOW, dtype=jnp.int32)[:, None]
    kj = jnp.arange(2 * WINDOW, dtype=jnp.int32)[None, :]
    dist = qi + WINDOW - kj
    in_win = (dist >= 0) & (dist < WINDOW)
    has_key = jnp.stack([in_win & (kj >= WINDOW), in_win])
    swa_bias = jnp.where(has_key[:, None], -(slopes.reshape(NH, 1, 1) * dist.astype(F32))[None], -jnp.inf)

    def head_cols(v, tq):
        return jnp.broadcast_to(v[:, :, None, :], (KV, G, tq, 1)).reshape(KV, G * tq, 1)

    x = jnp.concatenate([x_prompt.reshape(Mp, D), x_sample.reshape(Ms, D)], axis=0)
    p_all = jnp.concatenate([p_prompt.reshape(depth, Mp, -1), p_sample.reshape(depth, Ms, -1)],
                            axis=1).astype(BF16)
    gla_p = jnp.zeros((depth, Bp, H, DK, DV), F32)
    gla_s = jnp.zeros((depth, Bs, H, DK, DV), F32)
    ya = jnp.zeros((M, GV), BF16)
    yb = jnp.zeros((M, SQ), BF16)
    yc = jnp.zeros((M, PD), BF16)
    outs_p, outs_s = [], []
    h, ssq = _norm_prep(x, g_mix[0])

    for i in range(depth):
        b_a = b_gla_a[i].reshape(1, GK)
        g_o = g_gla_out[i].reshape(1, GV)
        p_scale = pool_scale[i].reshape(1, PD)
        sinks = swa_sinks[i].reshape(KV, G, 1)

        nxt = i + 1 if i + 1 < depth else None

        def side(w):
            return None if nxt is None else (w, nxt)

        z = _fused_matmul([h], [(0, w_main_b, i, 0)], [], _plain, NZ, F32, tm, tn_z, "in_proj",
                          scale=(ssq, (0,)))
        zlr = _fused_matmul([h], [(0, w_lr_b, i, 0)], [], _plain, LANES, F32, tm, LANES, "in_proj_lr",
                            scale=(ssq, (0,)))

        gla_args = (w_a2_b[i], b_a, g_o)
        gla_dims = (H, DK, DV, off_q, off_k, off_v, off_r)
        ya, gla_p = _gla(z, zlr, *gla_args, None, ya, gla_p, i, 0, Bp, Tp, *gla_dims)
        ya, gla_s = _gla(z, zlr, *gla_args, state_gla, ya, gla_s, i, Mp, Bs, Ts, *gla_dims)

        swa_dims = (KV, G, dh, off_sq, off_sk, off_sv)
        yb = _swa_blocks(z, swa_bias, swa_sinks[i], yb, Bp, Tp, *swa_dims)
        ck = cache_swa_k[i].reshape(Bs, WB, SKV)
        cv = cache_swa_v[i].reshape(Bs, WB, SKV)
        yb = _swa(z, ck, cv, head_cols(slopes, Ts), head_cols(sinks, Ts), yb, Mp, Bs, Ts, Ts, *swa_dims)

        yc = _pool(z, None, w_pool_b[i], p_scale, yc, 0, Bp, Tp, Tt_pool, off_u)
        pool_prefix = jnp.pad(state_pool[i], ((0, 0), (POOL_HALO - P, 0), (0, 0)))
        yc = _pool(z, pool_prefix, w_pool_b[i], p_scale, yc, Mp, Bs, Ts, Ts, off_u)

        m = _fused_matmul([h, ya, yb, yc],
                          [(0, w_gate_b, None, 0), (1, w_br_b[0], i, 0), (0, w_gate_b, None, D),
                           (2, w_br_b[1], i, 0), (0, w_gate_b, None, 2 * D), (3, w_br_b[2], i, 0)],
                          [], _gate_combine, D, BF16, tm_half, tn_g, "gate_combine",
                          scale=(ssq, (0, 2, 4)), cast=side(w_gate))
        if nxt is not None:
            m, w_gate_n = m
        x, h, ssq, *w_out_n = _fused_matmul([m], [(0, w_out_b, None, 0)], [x], _residual, D, F32, tm, tn_d,
                                            "out_proj", norm_g=g_ffn[i], cast=side(w_out))

        pre = state_ffn_conv[i]
        r0 = jnp.broadcast_to(pre[:, 0:1], (Bs, Ts, F)).reshape(Ms, F)
        r1 = jnp.broadcast_to(pre[:, 1:2], (Bs, Ts, F)).reshape(Ms, F)
        act, a_tail_s, a_tail_p, *w_up_n = _up_conv(h, ssq, w_up_b, w_conv, b_conv3, r0, r1, i, Mp, Tp, Ts,
                                                    tm, tf, side(w_up))
        x, h, ssq, *w_down_n = _residual_matmul_ksplit(act, w_down_b, x, g_ple[i], tm, tn_d, tk_down,
                                                       "down_proj", side(w_down))

        res = _fused_matmul([h, p_all[i]], [(0, w_pg_b, None, 0), (1, w_ple_b, i, 0)], [x], _ple, D, F32,
                            tm, tn_d, "ple", scale=(ssq, (0,)),
                            norm_g=None if nxt is None else g_mix[nxt], cast=side(w_ple_gate))
        if nxt is None:
            x = res
        else:
            x, h, ssq, w_pg_n = res
            w_gate_b, w_out_b, w_up_b, w_down_b, w_pg_b = w_gate_n, w_out_n[0], w_up_n[0], w_down_n[0], w_pg_n

        wb = min(WINDOW, Tp)

        def prompt_tail(n_rows, off, width):
            return jnp.stack([z[(b + 1) * Tp - n_rows:(b + 1) * Tp, off:off + width] for b in range(Bp)])

        zs = z[Mp:].reshape(Bs, Ts, NZ)
        swa_k_p = prompt_tail(wb, off_sk, SKV).reshape(Bp, wb, KV, dh)
        swa_v_p = prompt_tail(wb, off_sv, SKV).reshape(Bp, wb, KV, dh)
        pool_p = prompt_tail(P, off_u, PD)
        conv_p = jnp.stack([a_tail_p[((b + 1) * Tp - 1) // tm, SUBLANES - (CW - 1):] for b in range(Bp)])
        swa_k_s = jnp.concatenate([ck, zs[:, :, off_sk:off_sk + SKV]], axis=1)[:, Ts:].reshape(Bs, WB, KV, dh)
        swa_v_s = jnp.concatenate([cv, zs[:, :, off_sv:off_sv + SKV]], axis=1)[:, Ts:].reshape(Bs, WB, KV, dh)
        pool_s = jnp.concatenate([state_pool[i], zs[:, :, off_u:off_u + PD]], axis=1)[:, Ts:]
        conv_s = a_tail_s[M // tm - 1].reshape(Bs, Ts, F)[:, Ts - (CW - 1):]
        outs_p.append((swa_k_p, swa_v_p, pool_p, conv_p))
        outs_s.append((swa_k_s, swa_v_s, pool_s, conv_s))

    y_prompt = _rmsnorm(x, g_final, F32, 0, Mp).reshape(Bp, Tp, D)
    y_sample = _rmsnorm(x, g_final, F32, Mp, Ms).reshape(Bs, Ts, D)
    stack = lambda outs, c: jnp.stack([o[c] for o in outs])
    return (y_prompt, y_sample,
            gla_p, stack(outs_p, 0), stack(outs_p, 1), stack(outs_p, 2), stack(outs_p, 3),
            gla_s, stack(outs_s, 0), stack(outs_s, 1), stack(outs_s, 2), stack(outs_s, 3))
```
